```python
import math
import jax, jax.numpy as jnp
from jax import lax
import numpy as np

D_MODEL = 1024
BATCH = 32
SEQ = 2048
DEPTH = 4
DEC_BATCH = 32
DEC_SEQ = 32
PAST_LEN = 1024

CHUNK = 64
Q_BLOCK = 128
N_EVEN = (DEPTH + 1) // 2
N_ODD = DEPTH // 2
EPS = 1e-6
L2_EPS = 1e-6
NEG_INF = -1e30

MLA_HEADS = 8
MLA_NOPE = 64
MLA_ROPE = 32
MLA_V = 64
MLA_Q_RANK = 256
MLA_KV_RANK = 128
MLA_SCALE = (MLA_NOPE + MLA_ROPE) ** -0.5
ROPE_THETA = 10000.0
MLA_COLS = MLA_Q_RANK + MLA_KV_RANK + MLA_ROPE

RWKV_HEADS = 8
RWKV_HEAD = 64
RWKV_DIM = RWKV_HEADS * RWKV_HEAD
RWKV_DECAY_LORA = 64
RWKV_A_LORA = 64
RWKV_GATE_LORA = 128
RWKV_COLS = 3 * RWKV_DIM + RWKV_DECAY_LORA + RWKV_A_LORA + RWKV_GATE_LORA
RWKV_LN_EPS = 64e-5

EVEN_IN = MLA_COLS + RWKV_COLS
EVEN_MIX = MLA_HEADS * MLA_V + RWKV_DIM

POOL_GROUPS = 4
POOL_GROUP_DIM = 128
POOL_DIM = POOL_GROUPS * POOL_GROUP_DIM
POOL_WINDOWS = (2, 4, 8, 16)
POOL_HIST = 15

GDN_HEADS = 4
GDN_DK = 128
GDN_DV = 128
GDN_CONV = 4
GDN_QKV = GDN_HEADS * (2 * GDN_DK + GDN_DV)
GDN_DIM = GDN_HEADS * GDN_DV

ODD_IN = POOL_DIM + GDN_QKV + GDN_DIM + 2 * GDN_HEADS
ODD_MIX = POOL_DIM + GDN_DIM

D_FF = 2816
FFN_CONV = 3

kernel_name = 'hybrid_streaming_mla_rwkv7_pool_gdn_step'


def rms_norm(x, gain, eps=EPS):
    xf = x.astype(jnp.float32)
    y = xf * lax.rsqrt(jnp.mean(xf * xf, axis=-1, keepdims=True) + eps)
    return (y * gain.astype(jnp.float32)).astype(x.dtype)


def l2_normalize(x, eps=L2_EPS):
    xf = x.astype(jnp.float32)
    return (xf * lax.rsqrt(jnp.sum(xf * xf, axis=-1, keepdims=True) + eps)).astype(x.dtype)


def head_layer_norm(y, gain, bias, eps=RWKV_LN_EPS):
    B, T, H, N = y.shape
    yf = y.astype(jnp.float32)
    mu = jnp.mean(yf, axis=-1, keepdims=True)
    var = jnp.mean(jnp.square(yf - mu), axis=-1, keepdims=True)
    yn = ((yf - mu) * lax.rsqrt(var + eps)).reshape(B, T, H * N)
    return (yn * gain.astype(jnp.float32) + bias.astype(jnp.float32)).astype(y.dtype)


def modulate(x, gain, shift, scale):
    return rms_norm(x, gain) * (1.0 + scale[:, None, :]) + shift[:, None, :]


def rope_tables(pos):
    inv = 1.0 / (ROPE_THETA ** (jnp.arange(0, MLA_ROPE, 2, dtype=jnp.float32) / MLA_ROPE))
    ang = pos.astype(jnp.float32)[:, None] * inv[None, :]
    return jnp.cos(ang), jnp.sin(ang)


def apply_rope(x, cos, sin):
    x1, x2 = jnp.split(x.astype(jnp.float32), 2, axis=-1)
    return jnp.concatenate([x1 * cos - x2 * sin, x1 * sin + x2 * cos], axis=-1).astype(x.dtype)


def causal_dwconv(u, hist, w):
    T = u.shape[1]
    pad = jnp.concatenate([hist.astype(u.dtype), u], axis=1)
    out = w[0] * pad[:, 0:T]
    for i in range(1, w.shape[0]):
        out = out + w[i] * pad[:, i:i + T]
    return out, pad[:, T:]


def mla_attend(q_nope, q_pe, k_nope, k_pe, v, mask):
    s = (jnp.einsum('bqhd,bkhd->bhqk', q_nope, k_nope, preferred_element_type=jnp.float32)
         + jnp.einsum('bqhr,bkr->bhqk', q_pe, k_pe, preferred_element_type=jnp.float32)) * MLA_SCALE
    if mask is not None:
        s = jnp.where(mask, s, NEG_INF)
    p = jax.nn.softmax(s, axis=-1)
    return jnp.einsum('bhqk,bkhd->bqhd', p.astype(v.dtype), v)


def rwkv7_scan(S0, r, decay, k, v, kk, a):
    f32 = jnp.float32
    xs = tuple(jnp.moveaxis(t.astype(f32), 1, 0) for t in (r, decay, k, v, kk, a))

    def step(S, inp):
        r_t, w_t, k_t, v_t, kk_t, a_t = inp
        sa = jnp.einsum('bhvk,bhk->bhv', S, -kk_t)
        S = (S * w_t[:, :, None, :] + sa[..., None] * (kk_t * a_t)[:, :, None, :]
             + v_t[..., None] * k_t[:, :, None, :])
        return S, jnp.einsum('bhvk,bhk->bhv', S, r_t)

    S, ys = lax.scan(step, S0.astype(f32), xs)
    return jnp.moveaxis(ys, 0, 1).astype(r.dtype), S.astype(S0.dtype)


def gated_delta(S0, q, k, v, beta, g):
    f32 = jnp.float32
    B, T, H, _ = q.shape
    C = T if T <= CHUNK else CHUNK
    n = T // C

    def to_chunks(t):
        t = t.astype(f32).reshape((B, n, C) + t.shape[2:])
        return jnp.moveaxis(jnp.moveaxis(t, 1, 0), 3, 2)

    idx = jnp.arange(C)
    incl = idx[:, None] >= idx[None, :]
    strict = idx[:, None] > idx[None, :]
    eye = jnp.eye(C, dtype=f32)

    def step(S, inp):
        qc, kc, vc, bc, gc = inp
        G = jnp.cumsum(gc, axis=-1)
        diff = G[..., :, None] - G[..., None, :]
        dec_incl = jnp.where(incl, jnp.exp(jnp.where(incl, diff, 0.0)), 0.0)
        dec_strict = jnp.where(strict, dec_incl, 0.0)
        A = bc[..., None] * jnp.einsum('bhik,bhjk->bhij', kc, kc) * dec_strict
        eG = jnp.exp(G)
        rhs = bc[..., None] * (vc - eG[..., None] * jnp.einsum('bhck,bhkv->bhcv', kc, S))
        U = lax.linalg.triangular_solve(A + eye, rhs, left_side=True, lower=True, unit_diagonal=True)
        qk = jnp.einsum('bhik,bhjk->bhij', qc, kc) * dec_incl
        o = eG[..., None] * jnp.einsum('bhck,bhkv->bhcv', qc, S) + jnp.einsum('bhij,bhjv->bhiv', qk, U)
        decay_last = jnp.exp(G[..., -1:] - G)
        S = (jnp.exp(G[..., -1])[..., None, None] * S
             + jnp.einsum('bhck,bhcv->bhkv', kc * decay_last[..., None], U))
        return S, o

    S, o = lax.scan(step, S0.astype(f32), tuple(to_chunks(t) for t in (q, k, v, beta, g)))
    o = jnp.swapaxes(jnp.moveaxis(o, 0, 1), 2, 3).reshape(B, T, H, v.shape[-1])
    return o.astype(v.dtype), S.astype(S0.dtype)


def pool_mix(u, hist, pos0, pool_w, pool_scale):
    B, T, _ = u.shape
    f32 = jnp.float32
    xp = jnp.concatenate([hist.astype(u.dtype), u], axis=1)
    cs = jnp.concatenate([jnp.zeros((B, 1, POOL_DIM), f32), jnp.cumsum(xp.astype(f32), axis=1)], axis=1)
    end = cs[:, POOL_HIST + 1:]
    pos = pos0 + jnp.arange(T, dtype=jnp.int32)
    means = []
    for gi, w in enumerate(POOL_WINDOWS):
        sl = slice(gi * POOL_GROUP_DIM, (gi + 1) * POOL_GROUP_DIM)
        start = cs[:, POOL_HIST + 1 - w:POOL_HIST + 1 - w + T, sl]
        cnt = jnp.minimum(w, pos + 1).astype(f32)[None, :, None]
        means.append((end[..., sl] - start) / cnt)
    mean = jnp.stack(means, axis=2)
    diff = (mean - u.astype(f32).reshape(B, T, POOL_GROUPS, POOL_GROUP_DIM)).astype(u.dtype)
    y = jnp.einsum('btgc,gcd->btgd', diff, pool_w).reshape(B, T, POOL_DIM) * pool_scale
    return y, xp[:, T:]


def even_mixer(h, pos0, past_ckv, past_kpe, shift_state, wkv_state, W, i):
    B, T, _ = h.shape
    p = h @ W['even_w_in'][i]
    q_lat, kv_lat, kr_raw, rw = jnp.split(p, [MLA_Q_RANK, MLA_Q_RANK + MLA_KV_RANK, MLA_COLS], axis=-1)

    cos, sin = rope_tables(pos0 + jnp.arange(T, dtype=jnp.int32))
    q = (rms_norm(q_lat, W['mla_g_qlat'][i]) @ W['mla_w_uq'][i]).reshape(B, T, MLA_HEADS, MLA_NOPE + MLA_ROPE)
    q_nope = rms_norm(q[..., :MLA_NOPE], W['mla_g_qn'][i])
    q_pe = apply_rope(rms_norm(q[..., MLA_NOPE:], W['mla_g_qr'][i]), cos[:, None, :], sin[:, None, :])
    ckv_new = rms_norm(kv_lat, W['mla_g_kvlat'][i])
    kpe_new = apply_rope(rms_norm(kr_raw, W['mla_g_kr'][i]), cos, sin)
    if past_ckv is None:
        ckv_all, kpe_all = ckv_new, kpe_new
    else:
        ckv_all = jnp.concatenate([past_ckv.astype(ckv_new.dtype), ckv_new], axis=1)
        kpe_all = jnp.concatenate([past_kpe.astype(kpe_new.dtype), kpe_new], axis=1)
    L = ckv_all.shape[1]
    kv = (ckv_all @ W['mla_w_ukv'][i]).reshape(B, L, MLA_HEADS, MLA_NOPE + MLA_V)
    k_nope = rms_norm(kv[..., :MLA_NOPE], W['mla_g_kn'][i])
    v_mla = kv[..., MLA_NOPE:]
    if past_ckv is None:
        nb = T // Q_BLOCK
        qn_b = q_nope.reshape(B, nb, Q_BLOCK, MLA_HEADS, MLA_NOPE).transpose(1, 0, 2, 3, 4)
        qp_b = q_pe.reshape(B, nb, Q_BLOCK, MLA_HEADS, MLA_ROPE).transpose(1, 0, 2, 3, 4)
        k_chunk = jnp.arange(L) // CHUNK

        def block(args):
            qn_i, qp_i, bi = args
            q_chunk = (bi * Q_BLOCK + jnp.arange(Q_BLOCK)) // CHUNK
            mask = k_chunk[None, :] <= q_chunk[:, None]
            return mla_attend(qn_i, qp_i, k_nope, kpe_all, v_mla, mask)

        o_mla = lax.map(block, (qn_b, qp_b, jnp.arange(nb)))
        o_mla = o_mla.transpose(1, 0, 2, 3, 4).reshape(B, T, MLA_HEADS * MLA_V)
    else:
        o_mla = mla_attend(q_nope, q_pe, k_nope, kpe_all, v_mla, None).reshape(B, T, MLA_HEADS * MLA_V)

    prev = jnp.concatenate([shift_state[:, None, :].astype(rw.dtype), rw[:, :-1]], axis=1)
    xm = rw + (prev - rw) * W['rwkv_mu'][i]
    c0 = 3 * RWKV_DIM
    r, k, v, dw, da, dg = jnp.split(
        xm, [RWKV_DIM, 2 * RWKV_DIM, c0, c0 + RWKV_DECAY_LORA, c0 + RWKV_DECAY_LORA + RWKV_A_LORA], axis=-1)
    w_log = -jax.nn.softplus(-(W['rwkv_w0'][i] + jnp.tanh(dw) @ W['rwkv_w2'][i])) - 0.5
    a = jax.nn.sigmoid(W['rwkv_a0'][i] + da @ W['rwkv_a2'][i])
    g = jax.nn.sigmoid(dg) @ W['rwkv_g2'][i]

    def heads(t):
        return t.reshape(B, T, RWKV_HEADS, RWKV_HEAD)

    kk = l2_normalize(heads(k * W['rwkv_k_k'][i]))
    k = k * (1.0 + (a - 1.0) * W['rwkv_k_a'][i])
    r_h, k_h, v_h, a_h = heads(r), heads(k), heads(v), heads(a)
    decay = jnp.exp(-jnp.exp(heads(w_log).astype(jnp.float32)))
    y, wkv_new = rwkv7_scan(wkv_state, r_h, decay, k_h, v_h, kk, a_h)
    y = head_layer_norm(y, W['rwkv_lnx_g'][i], W['rwkv_lnx_b'][i])
    bonus = jnp.sum(r_h * k_h * W['rwkv_r_k'][i], axis=-1, keepdims=True) * v_h
    y = (y + bonus.reshape(B, T, RWKV_DIM)) * g

    out = jnp.concatenate([o_mla, y], axis=-1) @ W['even_w_out'][i]
    return out, ckv_new, kpe_new, rw[:, -1], wkv_new


def odd_mixer(h, pos0, pool_hist, conv_hist, gdn_state, W, i):
    B, T, _ = h.shape
    p = h @ W['odd_w_in'][i]
    c1 = POOL_DIM + GDN_QKV
    u, qkv, z, b, a = jnp.split(p, [POOL_DIM, c1, c1 + GDN_DIM, c1 + GDN_DIM + GDN_HEADS], axis=-1)

    y_pool, pool_new = pool_mix(u, pool_hist, pos0, W['pool_w'][i], W['pool_scale'][i])

    qkv_c, conv_new = causal_dwconv(qkv, conv_hist, W['gdn_conv_w'][i])
    qkv_c = jax.nn.silu(qkv_c)
    q, k, v = jnp.split(qkv_c, [GDN_HEADS * GDN_DK, 2 * GDN_HEADS * GDN_DK], axis=-1)
    q = l2_normalize(q.reshape(B, T, GDN_HEADS, GDN_DK)) * (GDN_DK ** -0.5)
    k = l2_normalize(k.reshape(B, T, GDN_HEADS, GDN_DK))
    v = v.reshape(B, T, GDN_HEADS, GDN_DV)
    beta = jax.nn.sigmoid(b.astype(jnp.float32))
    g = -jnp.exp(W['gdn_a_log'][i].astype(jnp.float32)) * jax.nn.softplus(
        a.astype(jnp.float32) + W['gdn_dt_bias'][i])
    o, gdn_new = gated_delta(gdn_state, q, k, v, beta, g)
    o = rms_norm(o, W['gdn_o_g'][i]) * jax.nn.silu(z.reshape(B, T, GDN_HEADS, GDN_DV))

    out = jnp.concatenate([y_pool, o.reshape(B, T, GDN_DIM)], axis=-1) @ W['odd_w_out'][i]
    return out, pool_new, conv_new, gdn_new


def conv_ffn(h, conv_hist, w_gate, w_up, conv_w, w_down):
    a, hist_new = causal_dwconv(h @ w_gate, conv_hist, conv_w)
    return (jax.nn.silu(a) * (h @ w_up)) @ w_down, hist_new


def trunk(x, c, pos0, caches, W):
    B = x.shape[0]
    dt = x.dtype
    if caches is None:
        ckv_c = kpe_c = None
        shift_c = jnp.zeros((N_EVEN, B, RWKV_COLS), dt)
        wkv_c = jnp.zeros((N_EVEN, B, RWKV_HEADS, RWKV_HEAD, RWKV_HEAD), dt)
        pool_c = jnp.zeros((N_ODD, B, POOL_HIST, POOL_DIM), dt)
        conv_c = jnp.zeros((N_ODD, B, GDN_CONV - 1, GDN_QKV), dt)
        gdn_c = jnp.zeros((N_ODD, B, GDN_HEADS, GDN_DK, GDN_DV), dt)
        ffn_c = jnp.zeros((DEPTH, B, FFN_CONV - 1, D_FF), dt)
    else:
        ckv_c, kpe_c, shift_c, wkv_c, pool_c, conv_c, gdn_c, ffn_c = caches
    mod = jnp.einsum('bd,lde->lbe', jax.nn.silu(c), W['ada_w']) + W['ada_b'][:, None, :]
    o_ckv, o_kpe, o_shift, o_wkv, o_pool, o_conv, o_gdn, o_ffn = [], [], [], [], [], [], [], []
    for layer in range(DEPTH):
        sh_m, sc_m, gt_m, sh_f, sc_f, gt_f = jnp.split(mod[layer], 6, axis=-1)
        h = modulate(x, W['norm_mix_g'][layer], sh_m, sc_m)
        i = layer // 2
        if layer % 2 == 0:
            y, ckv, kpe, sh, wkv = even_mixer(
                h, pos0, None if ckv_c is None else ckv_c[i], None if kpe_c is None else kpe_c[i],
                shift_c[i], wkv_c[i], W, i)
            o_ckv.append(ckv)
            o_kpe.append(kpe)
            o_shift.append(sh)
            o_wkv.append(wkv)
        else:
            y, pl, cv, gs = odd_mixer(h, pos0, pool_c[i], conv_c[i], gdn_c[i], W, i)
            o_pool.append(pl)
            o_conv.append(cv)
            o_gdn.append(gs)
        x = x + gt_m[:, None, :] * y
        h = modulate(x, W['norm_ffn_g'][layer], sh_f, sc_f)
        f, fc = conv_ffn(h, ffn_c[layer], W['ffn_w_gate'][layer], W['ffn_w_up'][layer],
                         W['ffn_conv_w'][layer], W['ffn_w_down'][layer])
        o_ffn.append(fc)
        x = x + gt_f[:, None, :] * f
    st = jnp.stack
    return x, (st(o_ckv), st(o_kpe), st(o_shift), st(o_wkv), st(o_pool), st(o_conv), st(o_gdn), st(o_ffn))


def setup_inputs(seed: int = 0) -> dict:
    key = jax.random.key(seed)
    ks = iter(jax.random.split(key, 80))
    f32 = jnp.float32

    def nrm(shape, scale=1.0):
        return jax.random.normal(next(ks), shape, f32) * scale

    def unif(shape, lo, hi):
        return jax.random.uniform(next(ks), shape, f32, lo, hi)

    def gain(shape, s=0.05):
        return 1.0 + nrm(shape, s)

    D = D_MODEL
    inp = {}
    inp['x_prompt'] = nrm((BATCH, SEQ, D))
    inp['x_sample'] = nrm((DEC_BATCH, DEC_SEQ, D))
    inp['cache_mla_ckv'] = nrm((N_EVEN, DEC_BATCH, PAST_LEN, MLA_KV_RANK))
    inp['cache_mla_kpe'] = nrm((N_EVEN, DEC_BATCH, PAST_LEN, MLA_ROPE))
    inp['state_rwkv_shift'] = nrm((N_EVEN, DEC_BATCH, RWKV_COLS))
    inp['state_rwkv_wkv'] = nrm((N_EVEN, DEC_BATCH, RWKV_HEADS, RWKV_HEAD, RWKV_HEAD), 0.3)
    inp['state_pool'] = nrm((N_ODD, DEC_BATCH, POOL_HIST, POOL_DIM))
    inp['state_gdn_conv'] = nrm((N_ODD, DEC_BATCH, GDN_CONV - 1, GDN_QKV))
    inp['state_gdn'] = nrm((N_ODD, DEC_BATCH, GDN_HEADS, GDN_DK, GDN_DV), 0.1)
    inp['state_ffn_conv'] = nrm((DEPTH, DEC_BATCH, FFN_CONV - 1, D_FF))
    inp['c_prompt'] = nrm((BATCH, D))
    inp['c_sample'] = nrm((DEC_BATCH, D))
    inp['ada_w'] = nrm((DEPTH, D, 6 * D), 0.5 * D ** -0.5)
    inp['ada_b'] = nrm((DEPTH, 6 * D), 0.01)
    inp['norm_mix_g'] = gain((DEPTH, D))
    inp['norm_ffn_g'] = gain((DEPTH, D))
    inp['even_w_in'] = nrm((N_EVEN, D, EVEN_IN), D ** -0.5)
    inp['mla_g_qlat'] = gain((N_EVEN, MLA_Q_RANK))
    inp['mla_g_kvlat'] = gain((N_EVEN, MLA_KV_RANK))
    inp['mla_w_uq'] = nrm((N_EVEN, MLA_Q_RANK, MLA_HEADS * (MLA_NOPE + MLA_ROPE)), MLA_Q_RANK ** -0.5)
    inp['mla_w_ukv'] = nrm((N_EVEN, MLA_KV_RANK, MLA_HEADS * (MLA_NOPE + MLA_V)), MLA_KV_RANK ** -0.5)
    inp['mla_g_qn'] = gain((N_EVEN, MLA_NOPE))
    inp['mla_g_qr'] = gain((N_EVEN, MLA_ROPE))
    inp['mla_g_kn'] = gain((N_EVEN, MLA_NOPE))
    inp['mla_g_kr'] = gain((N_EVEN, MLA_ROPE))
    inp['rwkv_mu'] = unif((N_EVEN, RWKV_COLS), 0.0, 1.0)
    inp['rwkv_w0'] = unif((N_EVEN, RWKV_DIM), -6.0, 0.0)
    inp['rwkv_w2'] = nrm((N_EVEN, RWKV_DECAY_LORA, RWKV_DIM), 0.5 * RWKV_DECAY_LORA ** -0.5)
    inp['rwkv_a0'] = nrm((N_EVEN, RWKV_DIM), 0.5)
    inp['rwkv_a2'] = nrm((N_EVEN, RWKV_A_LORA, RWKV_DIM), 0.5 * RWKV_A_LORA ** -0.5)
    inp['rwkv_g2'] = nrm((N_EVEN, RWKV_GATE_LORA, RWKV_DIM), RWKV_GATE_LORA ** -0.5)
    inp['rwkv_k_k'] = 0.85 + nrm((N_EVEN, RWKV_DIM), 0.05)
    inp['rwkv_k_a'] = gain((N_EVEN, RWKV_DIM))
    inp['rwkv_r_k'] = nrm((N_EVEN, RWKV_HEADS, RWKV_HEAD), 0.1)
    inp['rwkv_lnx_g'] = gain((N_EVEN, RWKV_DIM))
    inp['rwkv_lnx_b'] = nrm((N_EVEN, RWKV_DIM), 0.01)
    inp['even_w_out'] = nrm((N_EVEN, EVEN_MIX, D), EVEN_MIX ** -0.5)
    inp['odd_w_in'] = nrm((N_ODD, D, ODD_IN), D ** -0.5)
    inp['pool_w'] = nrm((N_ODD, POOL_GROUPS, POOL_GROUP_DIM, POOL_GROUP_DIM), POOL_GROUP_DIM ** -0.5)
    inp['pool_scale'] = gain((N_ODD, POOL_DIM), 0.1)
    inp['gdn_conv_w'] = nrm((N_ODD, GDN_CONV, GDN_QKV), GDN_CONV ** -0.5)
    inp['gdn_a_log'] = jnp.log(unif((N_ODD, GDN_HEADS), 1.0, 16.0))
    dt0 = jnp.exp(unif((N_ODD, GDN_HEADS), math.log(1e-3), math.log(1e-1)))
    inp['gdn_dt_bias'] = dt0 + jnp.log(-jnp.expm1(-dt0))
    inp['gdn_o_g'] = gain((N_ODD, GDN_DV))
    inp['odd_w_out'] = nrm((N_ODD, ODD_MIX, D), ODD_MIX ** -0.5)
    inp['ffn_w_gate'] = nrm((DEPTH, D, D_FF), D ** -0.5)
    inp['ffn_w_up'] = nrm((DEPTH, D, D_FF), D ** -0.5)
    inp['ffn_conv_w'] = nrm((DEPTH, FFN_CONV, D_FF), FFN_CONV ** -0.5)
    inp['ffn_w_down'] = nrm((DEPTH, D_FF, D), D_FF ** -0.5)
    return inp


def reference(x_prompt, x_sample, cache_mla_ckv, cache_mla_kpe, state_rwkv_shift, state_rwkv_wkv,
              state_pool, state_gdn_conv, state_gdn, state_ffn_conv, c_prompt, c_sample,
              ada_w, ada_b, norm_mix_g, norm_ffn_g,
              even_w_in, mla_g_qlat, mla_g_kvlat, mla_w_uq, mla_w_ukv, mla_g_qn, mla_g_qr, mla_g_kn, mla_g_kr,
              rwkv_mu, rwkv_w0, rwkv_w2, rwkv_a0, rwkv_a2, rwkv_g2, rwkv_k_k, rwkv_k_a, rwkv_r_k,
              rwkv_lnx_g, rwkv_lnx_b, even_w_out,
              odd_w_in, pool_w, pool_scale, gdn_conv_w, gdn_a_log, gdn_dt_bias, gdn_o_g, odd_w_out,
              ffn_w_gate, ffn_w_up, ffn_conv_w, ffn_w_down):
    W = {
        'ada_w': ada_w, 'ada_b': ada_b, 'norm_mix_g': norm_mix_g, 'norm_ffn_g': norm_ffn_g,
        'even_w_in': even_w_in, 'mla_g_qlat': mla_g_qlat, 'mla_g_kvlat': mla_g_kvlat,
        'mla_w_uq': mla_w_uq, 'mla_w_ukv': mla_w_ukv, 'mla_g_qn': mla_g_qn, 'mla_g_qr': mla_g_qr,
        'mla_g_kn': mla_g_kn, 'mla_g_kr': mla_g_kr,
        'rwkv_mu': rwkv_mu, 'rwkv_w0': rwkv_w0, 'rwkv_w2': rwkv_w2, 'rwkv_a0': rwkv_a0, 'rwkv_a2': rwkv_a2,
        'rwkv_g2': rwkv_g2, 'rwkv_k_k': rwkv_k_k, 'rwkv_k_a': rwkv_k_a, 'rwkv_r_k': rwkv_r_k,
        'rwkv_lnx_g': rwkv_lnx_g, 'rwkv_lnx_b': rwkv_lnx_b, 'even_w_out': even_w_out,
        'odd_w_in': odd_w_in, 'pool_w': pool_w, 'pool_scale': pool_scale, 'gdn_conv_w': gdn_conv_w,
        'gdn_a_log': gdn_a_log, 'gdn_dt_bias': gdn_dt_bias, 'gdn_o_g': gdn_o_g, 'odd_w_out': odd_w_out,
        'ffn_w_gate': ffn_w_gate, 'ffn_w_up': ffn_w_up, 'ffn_conv_w': ffn_conv_w, 'ffn_w_down': ffn_w_down,
    }
    y_prompt, p_states = trunk(x_prompt, c_prompt, 0, None, W)
    p_mla_ckv, p_mla_kpe, p_rwkv_shift, p_rwkv_wkv, p_pool, p_gdn_conv, p_gdn, p_ffn_conv = p_states
    past_len = cache_mla_ckv.shape[2]
    caches = (cache_mla_ckv, cache_mla_kpe, state_rwkv_shift, state_rwkv_wkv,
              state_pool, state_gdn_conv, state_gdn, state_ffn_conv)
    y_sample, s_states = trunk(x_sample, c_sample, past_len, caches, W)
    s_mla_ckv, s_mla_kpe, s_rwkv_shift, s_rwkv_wkv, s_pool, s_gdn_conv, s_gdn, s_ffn_conv = s_states
    return (y_prompt, y_sample,
            p_mla_ckv, p_mla_kpe, p_rwkv_shift, p_rwkv_wkv, p_pool, p_gdn_conv, p_gdn, p_ffn_conv,
            s_mla_ckv, s_mla_kpe, s_rwkv_shift, s_rwkv_wkv, s_pool, s_gdn_conv, s_gdn, s_ffn_conv)
```

```python
import functools
import math

import jax
import jax.numpy as jnp
from jax import lax
from jax.experimental import pallas as pl
from jax.experimental.pallas import tpu as pltpu

F32 = jnp.float32
BF16 = jnp.bfloat16

D_MODEL = 1024
DEPTH = 4
CHUNK = 64
EPS = 1e-6
L2_EPS = 1e-6
NEG_INF = -1e30

MLA_HEADS = 8
MLA_NOPE = 64
MLA_ROPE = 32
MLA_V = 64
MLA_Q_RANK = 256
MLA_KV_RANK = 128
MLA_SCALE = (MLA_NOPE + MLA_ROPE) ** -0.5
ROPE_THETA = 10000.0

RWKV_HEADS = 8
RWKV_HEAD = 64
RWKV_DIM = 512
RWKV_COLS = 1792
RWKV_LN_EPS = 64e-5

POOL_DIM = 512
POOL_WINDOWS = (2, 4, 8, 16)
POOL_HIST = 15

GDN_HEADS = 4
GDN_DK = 128
GDN_DV = 128
GDN_CONV = 4
GDN_QKV = 1536
GDN_DIM = 512

D_FF = 2816
FFN_CONV = 3

LANES = 128
HEAD_PAD = 128
VMEM_LIMIT = 56 * 1024 * 1024


def _dot(a, b):
    return jnp.dot(a, b, preferred_element_type=F32)


def _dot_nt(a, b):
    return lax.dot_general(a, b, (((1,), (1,)), ((), ())), preferred_element_type=F32)


def _dot_tn(a, b):
    return lax.dot_general(a, b, (((0,), (0,)), ((), ())), preferred_element_type=F32)


def _split(x, n):
    parts = []
    r = x
    for i in range(n):
        p = r.astype(BF16)
        parts.append(p)
        if i + 1 < n:
            r = r - p.astype(F32)
    return parts


def _sum_left(mat01, x, n):
    out = None
    for p in _split(x, n):
        d = _dot(mat01, p)
        out = d if out is None else out + d
    return out


def _sum_right(x, mat01, n):
    out = None
    for p in _split(x, n):
        d = _dot(p, mat01)
        out = d if out is None else out + d
    return out


def _sigmoid(x):
    return 1.0 / (1.0 + jnp.exp(-x))


def _silu(x):
    return x * _sigmoid(x)


def _softplus(x):
    return jnp.maximum(x, 0.0) + jnp.log(1.0 + jnp.exp(-jnp.abs(x)))


def _rms(x, gain, eps=EPS):
    return x * lax.rsqrt(jnp.mean(x * x, axis=-1, keepdims=True) + eps) * gain


def _modulate(x, gain, shift, scale):
    return _rms(x, gain) * (1.0 + scale) + shift


def _solve_unit_lower(n_mat, rhs, steps, refine=1):
    size = n_mat.shape[0]
    ri = lax.broadcasted_iota(jnp.int32, (size, size), 0)
    ci = lax.broadcasted_iota(jnp.int32, (size, size), 1)
    x = jnp.where(ri == ci, 1.0, 0.0) + n_mat
    p = n_mat
    for _ in range(steps):
        pb = p.astype(BF16)
        p = _dot(pb, pb)
        x = x + _dot(x.astype(BF16), p.astype(BF16))
    xb = x.astype(BF16)
    u = _dot(xb, rhs.astype(BF16))
    n_hi, n_lo = _split(n_mat, 2)
    for _ in range(refine):
        u_hi, u_lo = _split(u, 2)
        nu = _dot(n_hi, u_hi) + _dot(n_hi, u_lo) + _dot(n_lo, u_hi)
        u = u + _dot(xb, (rhs - u + nu).astype(BF16))
    return u


def _const_spec(shape):
    nd = len(shape)
    return pl.BlockSpec(shape, lambda *_: (0,) * nd, pipeline_mode=pl.Buffered(1))


def _params(n_axes):
    return pltpu.CompilerParams(dimension_semantics=("arbitrary",) * n_axes, vmem_limit_bytes=VMEM_LIMIT)


def _time_tile(t, pref):
    return pref if t % pref == 0 else t


def _ada_kernel(c_ref, w_ref, b_ref, o_ref):
    s = _silu(c_ref[...]).astype(BF16)
    o_ref[0] = _dot(s, w_ref[0].astype(BF16)) + b_ref[0]


def _ada(c_all, ada_w, ada_b):
    n = c_all.shape[0]
    tn = 1536
    return pl.pallas_call(
        _ada_kernel,
        out_shape=jax.ShapeDtypeStruct((DEPTH, n, 6 * D_MODEL), F32),
        grid=(DEPTH, 6 * D_MODEL // tn),
        in_specs=[
            pl.BlockSpec((n, D_MODEL), lambda l, j: (0, 0)),
            pl.BlockSpec((1, D_MODEL, tn), lambda l, j: (l, 0, j)),
            pl.BlockSpec((1, 1, tn), lambda l, j: (l, 0, j)),
        ],
        out_specs=pl.BlockSpec((1, n, tn), lambda l, j: (l, 0, j)),
        compiler_params=_params(2),
        name="ada_mod",
    )(c_all, ada_w, ada_b.reshape(DEPTH, 1, 6 * D_MODEL))


def _inproj_kernel(x_ref, mod_ref, g_ref, *refs, nseg):
    m = mod_ref[0]
    h = _modulate(x_ref[0], g_ref[...], m[0:1], m[1:2]).astype(BF16)
    for w_ref, o_ref in zip(refs[:nseg], refs[nseg:]):
        o_ref[0] = _dot(h, w_ref[...])


def _inproj(x, mod_l, gain, w_segs):
    b, t, d = x.shape
    tm = _time_tile(t, 512)
    nseg = len(w_segs)
    return pl.pallas_call(
        functools.partial(_inproj_kernel, nseg=nseg),
        out_shape=[jax.ShapeDtypeStruct((b, t, w.shape[1]), F32) for w in w_segs],
        grid=(b, t // tm),
        in_specs=[
            pl.BlockSpec((1, tm, d), lambda i, j: (i, j, 0)),
            pl.BlockSpec((1, 6, d), lambda i, j: (i, 0, 0)),
            _const_spec((1, d)),
        ] + [_const_spec(w.shape) for w in w_segs],
        out_specs=[pl.BlockSpec((1, tm, w.shape[1]), lambda i, j: (i, j, 0)) for w in w_segs],
        compiler_params=_params(2),
        name="in_proj",
    )(x, mod_l, gain, *w_segs)


def _rope(blk, c, sm, sp):
    return blk * c + pltpu.roll(blk, LANES - 16, 1) * sm + pltpu.roll(blk, 16, 1) * sp


def _mla_prep_kernel(p_ref, c_ref, sm_ref, sp_ref, gq_ref, gkv_ref, wuq_ref, ghead_ref, gkr_ref, gmat_ref,
                     q_ref, ckv_ref, kpe_ref, kblk_ref):
    p = p_ref[0]
    c, sm, sp = c_ref[...], sm_ref[...], sp_ref[...]
    qn = _rms(p[:, :MLA_Q_RANK], gq_ref[...]).astype(BF16)
    q = _dot(qn, wuq_ref[...])
    gmat = gmat_ref[...]
    for h in range(MLA_HEADS):
        hb = q[:, h * HEAD_PAD:(h + 1) * HEAD_PAD]
        ms = _sum_right(hb * hb, gmat, 2)
        hb = hb * lax.rsqrt(ms + EPS) * ghead_ref[...]
        q_ref[0, h] = _rope(hb, c, sm, sp).astype(BF16)
    ckv_ref[0] = _rms(p[:, MLA_Q_RANK:MLA_Q_RANK + MLA_KV_RANK], gkv_ref[...])
    kr = p[:, MLA_Q_RANK + MLA_KV_RANK:]
    ms = jnp.sum(kr * kr, axis=-1, keepdims=True) * (1.0 / MLA_ROPE)
    kb = _rope(kr * lax.rsqrt(ms + EPS) * gkr_ref[...], c, sm, sp)
    kblk_ref[0] = kb
    kpe_ref[0] = kb[:, MLA_NOPE:MLA_NOPE + MLA_ROPE]


def _mla_prep(p_mla, tabs, wl):
    b, t, _ = p_mla.shape
    tm = _time_tile(t, 512)
    tab_spec = pl.BlockSpec((tm, LANES), lambda i, j: (j, 0))
    return pl.pallas_call(
        _mla_prep_kernel,
        out_shape=[
            jax.ShapeDtypeStruct((b, MLA_HEADS, t, HEAD_PAD), BF16),
            jax.ShapeDtypeStruct((b, t, MLA_KV_RANK), F32),
            jax.ShapeDtypeStruct((b, t, MLA_ROPE), F32),
            jax.ShapeDtypeStruct((b, t, LANES), F32),
        ],
        grid=(b, t // tm),
        in_specs=[
            pl.BlockSpec((1, tm, 512), lambda i, j: (i, j, 0)),
            tab_spec, tab_spec, tab_spec,
            _const_spec((1, MLA_Q_RANK)), _const_spec((1, MLA_KV_RANK)),
            _const_spec((MLA_Q_RANK, MLA_HEADS * HEAD_PAD)),
            _const_spec((1, LANES)), _const_spec((1, LANES)), _const_spec((LANES, LANES)),
        ],
        out_specs=[
            pl.BlockSpec((1, MLA_HEADS, tm, HEAD_PAD), lambda i, j: (i, 0, j, 0)),
            pl.BlockSpec((1, tm, MLA_KV_RANK), lambda i, j: (i, j, 0)),
            pl.BlockSpec((1, tm, MLA_ROPE), lambda i, j: (i, j, 0)),
            pl.BlockSpec((1, tm, LANES), lambda i, j: (i, j, 0)),
        ],
        compiler_params=_params(2),
        name="mla_prep",
    )(p_mla, tabs[0], tabs[1], tabs[2], wl["g_qlat"], wl["g_kvlat"], wl["w_uq"], wl["g_qhead"], wl["g_kr"],
      wl["gmat_q"])


def _mla_kv_kernel(ckv_ref, kpe_ref, wuk_ref, wuv_ref, gkn_ref, k_ref, v_ref):
    cb = ckv_ref[0].astype(BF16)
    kf = _dot(cb, wuk_ref[...])
    vf = _dot(cb, wuv_ref[...])
    kpe = kpe_ref[0]
    for h in range(MLA_HEADS):
        kb = kf[:, h * HEAD_PAD:(h + 1) * HEAD_PAD]
        ms = jnp.sum(kb * kb, axis=-1, keepdims=True) * (1.0 / MLA_NOPE)
        k_ref[0, h] = (kb * lax.rsqrt(ms + EPS) * gkn_ref[...] + kpe).astype(BF16)
        v_ref[0, h] = vf[:, h * HEAD_PAD:(h + 1) * HEAD_PAD].astype(BF16)


def _mla_kv(ckv_all, kpe_blk, wl):
    b, l, _ = ckv_all.shape
    tl = _time_tile(l, 512)
    return pl.pallas_call(
        _mla_kv_kernel,
        out_shape=[jax.ShapeDtypeStruct((b, MLA_HEADS, l, HEAD_PAD), BF16)] * 2,
        grid=(b, l // tl),
        in_specs=[
            pl.BlockSpec((1, tl, MLA_KV_RANK), lambda i, j: (i, j, 0)),
            pl.BlockSpec((1, tl, LANES), lambda i, j: (i, j, 0)),
            _const_spec((MLA_KV_RANK, MLA_HEADS * HEAD_PAD)),
            _const_spec((MLA_KV_RANK, MLA_HEADS * HEAD_PAD)),
            _const_spec((1, LANES)),
        ],
        out_specs=[pl.BlockSpec((1, MLA_HEADS, tl, HEAD_PAD), lambda i, j: (i, 0, j, 0))] * 2,
        compiler_params=_params(2),
        name="mla_kv",
    )(ckv_all, kpe_blk, wl["w_uk"], wl["w_uv"], wl["g_kn"])


def _attn_kernel(q_ref, k_ref, v_ref, o_ref, *, tq, tk, past_len):
    d0 = pl.multiple_of(past_len + pl.program_id(1) * tq, tq)
    nprefix = d0 // tk
    qc = lax.broadcasted_iota(jnp.int32, (tq, tq), 0) // CHUNK
    kc = lax.broadcasted_iota(jnp.int32, (tq, tq), 1) // CHUNK
    mask = kc <= qc
    outs = []
    for h in range(MLA_HEADS):
        q = q_ref[0, h]
        s = _dot_nt(q, k_ref[0, h, pl.ds(d0, tq), :]) * MLA_SCALE
        s = jnp.where(mask, s, NEG_INF)
        m = jnp.max(s, axis=-1, keepdims=True)
        p = jnp.exp(s - m)
        l = jnp.sum(p, axis=-1, keepdims=True)
        acc = _dot(p.astype(BF16), v_ref[0, h, pl.ds(d0, tq), :])

        def body(j, carry, q=q, h=h):
            m, l, acc = carry
            r0 = pl.multiple_of(j * tk, tk)
            s = _dot_nt(q, k_ref[0, h, pl.ds(r0, tk), :]) * MLA_SCALE
            mn = jnp.maximum(m, jnp.max(s, axis=-1, keepdims=True))
            alpha = jnp.exp(m - mn)
            p = jnp.exp(s - mn)
            l = alpha * l + jnp.sum(p, axis=-1, keepdims=True)
            acc = alpha * acc + _dot(p.astype(BF16), v_ref[0, h, pl.ds(r0, tk), :])
            return mn, l, acc

        m, l, acc = lax.fori_loop(0, nprefix, body, (m, l, acc))
        outs.append(acc / l)
    for j in range(MLA_HEADS // 2):
        o_ref[0, :, j * LANES:(j + 1) * LANES] = outs[2 * j] + outs[2 * j + 1]


def _attention(q, k, v, past_len):
    b, hh, t, _ = q.shape
    l = k.shape[2]
    tq = _time_tile(t, 256)
    tk = 256
    assert past_len % tk == 0 and (tq % tk == 0 or past_len + tq == l) and l == past_len + t
    return pl.pallas_call(
        functools.partial(_attn_kernel, tq=tq, tk=tk, past_len=past_len),
        out_shape=jax.ShapeDtypeStruct((b, t, MLA_HEADS * MLA_V), F32),
        grid=(b, t // tq),
        in_specs=[
            pl.BlockSpec((1, hh, tq, HEAD_PAD), lambda i, j: (i, 0, j, 0)),
            pl.BlockSpec((1, hh, l, HEAD_PAD), lambda i, j: (i, 0, 0, 0)),
            pl.BlockSpec((1, hh, l, HEAD_PAD), lambda i, j: (i, 0, 0, 0)),
        ],
        out_specs=pl.BlockSpec((1, tq, MLA_HEADS * MLA_V), lambda i, j: (i, j, 0)),
        compiler_params=_params(2),
        name="mla_attn",
    )(q, k, v)


def _rwkv_kernel(rw_ref, sh_ref, s0_ref, mu_ref, w0_ref, w2_ref, a0_ref, a2_ref, g2_ref, kkw_ref, ka_ref,
                 rk_ref, lng_ref, lnb_ref, gsum_ref, y_ref, s_ref, xs_scr, st_scr, *, tt, cc):
    t = pl.program_id(1)

    @pl.when(t == 0)
    def _():
        xs_scr[7:8, :] = sh_ref[0]
        st_scr[...] = s0_ref[0]

    rw = rw_ref[0]
    xs_scr[8:8 + tt, :] = rw
    prev = xs_scr[7:7 + tt, :]
    xm = rw + (prev - rw) * mu_ref[...]
    xs_scr[7:8, :] = rw[tt - 1:tt, :]

    r = xm[:, 0:RWKV_DIM]
    k = xm[:, RWKV_DIM:2 * RWKV_DIM]
    v = xm[:, 2 * RWKV_DIM:3 * RWKV_DIM]
    lw = xm[:, 3 * RWKV_DIM:3 * RWKV_DIM + LANES]
    dg = xm[:, 3 * RWKV_DIM + LANES:]
    zw = w0_ref[...] + _dot(jnp.tanh(lw).astype(BF16), w2_ref[...])
    ld = -jnp.exp(-_softplus(-zw) - 0.5)
    a = _sigmoid(a0_ref[...] + _dot(lw.astype(BF16), a2_ref[...]))
    g = _dot(_sigmoid(dg).astype(BF16), g2_ref[...])

    gmat = gsum_ref[...]

    def gsum(x):
        return jnp.concatenate([_sum_right(x[:, :256], gmat, 2), _sum_right(x[:, 256:], gmat, 2)], axis=-1)

    kkr = k * kkw_ref[...]
    kk = kkr * lax.rsqrt(gsum(kkr * kkr) + L2_EPS)
    k2 = k * (1.0 + (a - 1.0) * ka_ref[...])
    kka = kk * a
    bonus = gsum(r * k2 * rk_ref[...]) * v

    c2 = 2 * cc
    ri = lax.broadcasted_iota(jnp.int32, (cc, cc), 0)
    ci = lax.broadcasted_iota(jnp.int32, (cc, cc), 1)
    tri = jnp.where(ci <= ri, 1.0, 0.0).astype(BF16)
    rl = lax.broadcasted_iota(jnp.int32, (c2, 2 * c2), 0) % cc
    cl = lax.broadcasted_iota(jnp.int32, (c2, 2 * c2), 1) % cc
    strict = cl < rl
    incl = cl <= rl
    low = lax.broadcasted_iota(jnp.int32, (1, LANES), 1) < RWKV_HEAD

    def blk(x):
        return jnp.concatenate([jnp.where(low, x, 0.0), jnp.where(low, 0.0, x)], axis=0)

    ys = []
    for c in range(tt // cc):
        rows = slice(c * cc, (c + 1) * cc)
        ldc = ld[rows]
        cs = _sum_left(tri, ldc, 3)
        tot = cs[cc - 1:cc, :]
        pin = jnp.exp(cs)
        pex = jnp.exp(cs - ldc)
        pinv = jnp.exp(-cs)
        pend = jnp.exp(tot - cs)
        ptot = jnp.exp(tot)
        at = -(kk[rows] * pex)
        rt = r[rows] * pin
        bt = kka[rows] * pinv
        kt = k2[rows] * pinv
        bh = kka[rows] * pend
        kh = k2[rows] * pend
        vc = v[rows]
        yp = []
        for j in range(RWKV_HEADS // 2):
            ls = slice(j * LANES, (j + 1) * LANES)
            v_b = blk(vc[:, ls])
            lhs = jnp.concatenate([blk(at[:, ls]), blk(rt[:, ls])], axis=0).astype(BF16)
            rhs = jnp.concatenate([blk(bt[:, ls]), blk(kt[:, ls])], axis=0).astype(BF16)
            mm = _dot_nt(lhs, rhs)
            sb = st_scr[j]
            ah = _dot_nt(lhs, sb.astype(BF16))
            mt = jnp.where(strict, mm[:c2], 0.0)
            mb = jnp.where(incl, mm[c2:], 0.0)
            zv = jnp.concatenate([jnp.zeros_like(v_b), v_b], axis=0).astype(BF16)
            rhs_u = ah[:c2] + _dot(mt.astype(BF16), zv)
            u_b = _solve_unit_lower(mt[:, :c2], rhs_u, int(math.log2(cc)) - 1)
            uv = jnp.concatenate([u_b, v_b], axis=0).astype(BF16)
            y_b = ah[c2:] + _dot(mb.astype(BF16), uv)
            yp.append(y_b[:cc] + y_b[cc:])
            bk = jnp.concatenate([blk(bh[:, ls]), blk(kh[:, ls])], axis=0).astype(BF16)
            st_scr[j] = sb * ptot[:, ls] + _dot_tn(uv, bk)
        ys.append(jnp.concatenate(yp, axis=-1))
    y = jnp.concatenate(ys, axis=0) if len(ys) > 1 else ys[0]

    mu = gsum(y) * (1.0 / RWKV_HEAD)
    dy = y - mu
    var = gsum(dy * dy) * (1.0 / RWKV_HEAD)
    yn = dy * lax.rsqrt(var + RWKV_LN_EPS) * lng_ref[...] + lnb_ref[...]
    y_ref[0] = (yn + bonus) * g
    s_ref[0] = st_scr[...]


def _rwkv(rw, shift, sblk, wl):
    b, t, _ = rw.shape
    cc = min(t, CHUNK)
    tt = _time_tile(t, 256)
    row = lambda n: _const_spec((1, n))
    return pl.pallas_call(
        functools.partial(_rwkv_kernel, tt=tt, cc=cc),
        out_shape=[
            jax.ShapeDtypeStruct((b, t, RWKV_DIM), F32),
            jax.ShapeDtypeStruct((b, RWKV_HEADS // 2, LANES, LANES), F32),
        ],
        grid=(b, t // tt),
        in_specs=[
            pl.BlockSpec((1, tt, RWKV_COLS), lambda i, j: (i, j, 0)),
            pl.BlockSpec((1, 1, RWKV_COLS), lambda i, j: (i, 0, 0)),
            pl.BlockSpec((1, RWKV_HEADS // 2, LANES, LANES), lambda i, j: (i, 0, 0, 0)),
            row(RWKV_COLS), row(RWKV_DIM), _const_spec((LANES, RWKV_DIM)), row(RWKV_DIM),
            _const_spec((LANES, RWKV_DIM)), _const_spec((LANES, RWKV_DIM)),
            row(RWKV_DIM), row(RWKV_DIM), row(RWKV_DIM), row(RWKV_DIM), row(RWKV_DIM),
            _const_spec((256, 256)),
        ],
        out_specs=[
            pl.BlockSpec((1, tt, RWKV_DIM), lambda i, j: (i, j, 0)),
            pl.BlockSpec((1, RWKV_HEADS // 2, LANES, LANES), lambda i, j: (i, 0, 0, 0)),
        ],
        scratch_shapes=[
            pltpu.VMEM((tt + 8, RWKV_COLS), F32),
            pltpu.VMEM((RWKV_HEADS // 2, LANES, LANES), F32),
        ],
        compiler_params=_params(2),
        name="rwkv7",
    )(rw, shift, sblk, wl["mu"], wl["w0"], wl["w2"], wl["a0"], wl["a2"], wl["g2"], wl["k_k"], wl["k_a"],
      wl["r_k"], wl["lnx_g"], wl["lnx_b"], wl["gsum"])


def _pool_kernel(u_ref, hist_ref, pw_ref, ps_ref, o_ref, scr, *, tt, pos0):
    t = pl.program_id(1)

    @pl.when(t == 0)
    def _():
        scr[0:16, :] = hist_ref[0]

    u = u_ref[0]
    scr[16:16 + tt, :] = u
    pos = pos0 + t * tt + lax.broadcasted_iota(jnp.int32, (tt, 1), 0)
    for gi, w in enumerate(POOL_WINDOWS):
        ls = slice(gi * LANES, (gi + 1) * LANES)
        ug = u[:, ls]
        s = ug
        for d in range(1, w):
            s = s + scr[16 - d:16 - d + tt, ls]
        cnt = jnp.minimum(w, pos + 1).astype(F32)
        diff = s / cnt - ug
        o_ref[0, :, ls] = _dot(diff.astype(BF16), pw_ref[gi]) * ps_ref[:, ls]
    scr[0:16, :] = scr[tt:tt + 16, :]


def _pool(u, hist16, pos0, wl):
    b, t, _ = u.shape
    tt = _time_tile(t, 256)
    return pl.pallas_call(
        functools.partial(_pool_kernel, tt=tt, pos0=pos0),
        out_shape=jax.ShapeDtypeStruct((b, t, POOL_DIM), F32),
        grid=(b, t // tt),
        in_specs=[
            pl.BlockSpec((1, tt, POOL_DIM), lambda i, j: (i, j, 0)),
            pl.BlockSpec((1, 16, POOL_DIM), lambda i, j: (i, 0, 0)),
            _const_spec((len(POOL_WINDOWS), LANES, LANES)),
            _const_spec((1, POOL_DIM)),
        ],
        out_specs=pl.BlockSpec((1, tt, POOL_DIM), lambda i, j: (i, j, 0)),
        scratch_shapes=[pltpu.VMEM((tt + 16, POOL_DIM), F32)],
        compiler_params=_params(2),
        name="pool_mix",
    )(u, hist16, wl["pool_w"], wl["pool_scale"])


def _gdn_kernel(qkv_ref, z_ref, ba_ref, hist_ref, s0_ref, cw_ref, alog_ref, dtb_ref, og_ref, sel_ref,
                o_ref, s_ref, x_scr, st_scr, *, tt, cc):
    t = pl.program_id(1)

    @pl.when(t == 0)
    def _():
        x_scr[5:8, :] = hist_ref[0]
        st_scr[...] = s0_ref[0]

    x = qkv_ref[0]
    x_scr[8:8 + tt, :] = x
    cw = cw_ref[...]
    xc = cw[3:4] * x + cw[2:3] * x_scr[7:7 + tt, :] + cw[1:2] * x_scr[6:6 + tt, :] + cw[0:1] * x_scr[5:5 + tt, :]
    x_scr[5:8, :] = x[tt - 3:tt, :]
    xc = _silu(xc)
    ba = ba_ref[0]
    beta_all = _sigmoid(ba)
    g_all = -jnp.exp(alog_ref[...]) * _softplus(ba + dtb_ref[...])

    ri = lax.broadcasted_iota(jnp.int32, (cc, cc), 0)
    ci = lax.broadcasted_iota(jnp.int32, (cc, cc), 1)
    incl = ci <= ri
    strict = ci < ri
    tri = jnp.where(incl, 1.0, 0.0).astype(BF16)
    sel = sel_ref[...]
    zz = z_ref[0]

    os_ = []
    for c in range(tt // cc):
        rows = slice(c * cc, (c + 1) * cc)
        gcs = _sum_left(tri, g_all[rows], 3)
        parts = _split(gcs, 3)
        grow = _dot_nt(sel, parts[0]) + _dot_nt(sel, parts[1]) + _dot_nt(sel, parts[2])
        oh = []
        for h in range(GDN_HEADS):
            qh = xc[rows, h * GDN_DK:(h + 1) * GDN_DK]
            kh = xc[rows, GDN_HEADS * GDN_DK + h * GDN_DK:GDN_HEADS * GDN_DK + (h + 1) * GDN_DK]
            vh = xc[rows, 2 * GDN_HEADS * GDN_DK + h * GDN_DV:2 * GDN_HEADS * GDN_DK + (h + 1) * GDN_DV]
            qh = qh * lax.rsqrt(jnp.sum(qh * qh, axis=-1, keepdims=True) + L2_EPS) * (GDN_DK ** -0.5)
            kh = kh * lax.rsqrt(jnp.sum(kh * kh, axis=-1, keepdims=True) + L2_EPS)
            beta = beta_all[rows, h:h + 1]
            gc = gcs[:, GDN_HEADS + h:GDN_HEADS + h + 1]
            gr = grow[h:h + 1, :]
            glast = gc[cc - 1:cc, :]
            dec_incl = jnp.where(incl, jnp.exp(jnp.where(incl, gc - gr, 0.0)), 0.0)
            dec_strict = jnp.where(strict, dec_incl, 0.0)
            kb = kh.astype(BF16)
            kq = jnp.concatenate([kh, qh], axis=0).astype(BF16)
            gram = _dot_nt(kq, kb)
            sst = st_scr[h]
            ks = _dot(kq, sst.astype(BF16))
            eg = jnp.exp(gc)
            n_mat = -(beta * gram[:cc] * dec_strict)
            rhs = beta * (vh - eg * ks[:cc])
            u = _solve_unit_lower(n_mat, rhs, int(math.log2(cc)) - 1)
            ub = u.astype(BF16)
            o = eg * ks[cc:] + _dot((gram[cc:] * dec_incl).astype(BF16), ub)
            st_scr[h] = jnp.exp(glast) * sst + _dot_tn((kh * jnp.exp(glast - gc)).astype(BF16), ub)
            zh = zz[rows, h * GDN_DV:(h + 1) * GDN_DV]
            oh.append(_rms(o, og_ref[...]) * _silu(zh))
        os_.append(jnp.concatenate(oh, axis=-1))
    o_ref[0] = jnp.concatenate(os_, axis=0) if len(os_) > 1 else os_[0]
    s_ref[0] = st_scr[...]


def _gdn(qkv, z, ba, conv_hist, state, wl):
    b, t, _ = qkv.shape
    cc = min(t, CHUNK)
    tt = _time_tile(t, 256)
    return pl.pallas_call(
        functools.partial(_gdn_kernel, tt=tt, cc=cc),
        out_shape=[
            jax.ShapeDtypeStruct((b, t, GDN_DIM), F32),
            jax.ShapeDtypeStruct((b, GDN_HEADS, GDN_DK, GDN_DV), F32),
        ],
        grid=(b, t // tt),
        in_specs=[
            pl.BlockSpec((1, tt, GDN_QKV), lambda i, j: (i, j, 0)),
            pl.BlockSpec((1, tt, GDN_DIM), lambda i, j: (i, j, 0)),
            pl.BlockSpec((1, tt, LANES), lambda i, j: (i, j, 0)),
            pl.BlockSpec((1, GDN_CONV - 1, GDN_QKV), lambda i, j: (i, 0, 0)),
            pl.BlockSpec((1, GDN_HEADS, GDN_DK, GDN_DV), lambda i, j: (i, 0, 0, 0)),
            _const_spec((GDN_CONV, GDN_QKV)),
            _const_spec((1, LANES)), _const_spec((1, LANES)), _const_spec((1, GDN_DV)),
            _const_spec((8, LANES)),
        ],
        out_specs=[
            pl.BlockSpec((1, tt, GDN_DIM), lambda i, j: (i, j, 0)),
            pl.BlockSpec((1, GDN_HEADS, GDN_DK, GDN_DV), lambda i, j: (i, 0, 0, 0)),
        ],
        scratch_shapes=[
            pltpu.VMEM((tt + 8, GDN_QKV), F32),
            pltpu.VMEM((GDN_HEADS, GDN_DK, GDN_DV), F32),
        ],
        compiler_params=_params(2),
        name="gdn",
    )(qkv, z, ba, conv_hist, state, wl["conv_w"], wl["a_log"], wl["dt_bias"], wl["o_g"], wl["sel"])


def _outproj_kernel(x_ref, m1_ref, m2_ref, mod_ref, w1_ref, w2_ref, o_ref):
    y = _dot(m1_ref[0].astype(BF16), w1_ref[...]) + _dot(m2_ref[0].astype(BF16), w2_ref[...])
    o_ref[0] = x_ref[0] + mod_ref[0][2:3] * y


def _outproj(x, m1, m2, mod_l, w1, w2):
    b, t, d = x.shape
    tm = _time_tile(t, 512)
    half = m1.shape[2]
    return pl.pallas_call(
        _outproj_kernel,
        out_shape=jax.ShapeDtypeStruct((b, t, d), F32),
        grid=(b, t // tm),
        in_specs=[
            pl.BlockSpec((1, tm, d), lambda i, j: (i, j, 0)),
            pl.BlockSpec((1, tm, half), lambda i, j: (i, j, 0)),
            pl.BlockSpec((1, tm, half), lambda i, j: (i, j, 0)),
            pl.BlockSpec((1, 6, d), lambda i, j: (i, 0, 0)),
            _const_spec((half, d)), _const_spec((half, d)),
        ],
        out_specs=pl.BlockSpec((1, tm, d), lambda i, j: (i, j, 0)),
        compiler_params=_params(2),
        name="out_proj",
    )(x, m1, m2, mod_l, w1, w2)


def _ffn_kernel(x_ref, mod_ref, g_ref, hist_ref, wg_ref, wu_ref, cw_ref, wd_ref, o_ref, hn_ref, a_scr, *, tm, nfc):
    t = pl.program_id(1)

    @pl.when(t == 0)
    def _():
        a_scr[6:8, :] = hist_ref[0]

    x = x_ref[0]
    m = mod_ref[0]
    h = _modulate(x, g_ref[...], m[3:4], m[4:5]).astype(BF16)
    fc = D_FF // nfc
    acc = None
    for c in range(nfc):
        cs = slice(c * fc, (c + 1) * fc)
        a = _dot(h, wg_ref[:, cs])
        a_scr[8:8 + tm, cs] = a
        ac = cw_ref[2:3, cs] * a + cw_ref[1:2, cs] * a_scr[7:7 + tm, cs] + cw_ref[0:1, cs] * a_scr[6:6 + tm, cs]
        a_scr[6:8, cs] = a[tm - 2:tm, :]
        act = (_silu(ac) * _dot(h, wu_ref[:, cs])).astype(BF16)
        d = _dot(act, wd_ref[cs, :])
        acc = d if acc is None else acc + d
    o_ref[0] = x + m[5:6] * acc
    hn_ref[0] = a_scr[6:8, :]


def _ffn(x, mod_l, gain, hist, wg, wu, cw, wd):
    b, t, d = x.shape
    tm = _time_tile(t, 256)
    return pl.pallas_call(
        functools.partial(_ffn_kernel, tm=tm, nfc=2),
        out_shape=[
            jax.ShapeDtypeStruct((b, t, d), F32),
            jax.ShapeDtypeStruct((b, FFN_CONV - 1, D_FF), F32),
        ],
        grid=(b, t // tm),
        in_specs=[
            pl.BlockSpec((1, tm, d), lambda i, j: (i, j, 0)),
            pl.BlockSpec((1, 6, d), lambda i, j: (i, 0, 0)),
            _const_spec((1, d)),
            pl.BlockSpec((1, FFN_CONV - 1, D_FF), lambda i, j: (i, 0, 0)),
            _const_spec((d, D_FF)), _const_spec((d, D_FF)), _const_spec((FFN_CONV, D_FF)), _const_spec((D_FF, d)),
        ],
        out_specs=[
            pl.BlockSpec((1, tm, d), lambda i, j: (i, j, 0)),
            pl.BlockSpec((1, FFN_CONV - 1, D_FF), lambda i, j: (i, 0, 0)),
        ],
        scratch_shapes=[pltpu.VMEM((tm + 8, D_FF), F32)],
        compiler_params=_params(2),
        name="conv_ffn",
    )(x, mod_l, gain, hist, wg, wu, cw, wd)


def _pad_cols(w, n):
    return jnp.pad(w, ((0, 0), (0, n - w.shape[1])))


def _lane_row(vec, offset):
    return jnp.pad(vec, (offset, LANES - offset - vec.shape[0])).reshape(1, LANES)


def _prep_even(W, i):
    f = lambda name: W[name][i]
    w_in = f("even_w_in")
    mla_cols = MLA_Q_RANK + MLA_KV_RANK
    kr_w = jnp.pad(w_in[:, mla_cols:mla_cols + MLA_ROPE], ((0, 0), (MLA_NOPE, LANES - MLA_NOPE - MLA_ROPE)))
    w_mla = jnp.concatenate([w_in[:, :mla_cols], kr_w], axis=1).astype(BF16)
    w_rw = w_in[:, mla_cols + MLA_ROPE:].astype(BF16)
    per_q = MLA_NOPE + MLA_ROPE
    w_uq = f("mla_w_uq").reshape(MLA_Q_RANK, MLA_HEADS, per_q)
    w_uq = jnp.pad(w_uq, ((0, 0), (0, 0), (0, HEAD_PAD - per_q))).reshape(MLA_Q_RANK, MLA_HEADS * HEAD_PAD)
    w_ukv = f("mla_w_ukv").reshape(MLA_KV_RANK, MLA_HEADS, MLA_NOPE + MLA_V)
    w_uk = jnp.pad(w_ukv[:, :, :MLA_NOPE], ((0, 0), (0, 0), (0, HEAD_PAD - MLA_NOPE)))
    vv = w_ukv[:, :, MLA_NOPE:]
    v_even = jnp.pad(vv, ((0, 0), (0, 0), (0, HEAD_PAD - MLA_V)))
    v_odd = jnp.pad(vv, ((0, 0), (0, 0), (HEAD_PAD - MLA_V, 0)))
    odd = (jnp.arange(MLA_HEADS) % 2 == 1)[None, :, None]
    w_uv = jnp.where(odd, v_odd, v_even)
    gi = jnp.arange(LANES)
    grp = jnp.where(gi < MLA_NOPE, 0, jnp.where(gi < per_q, 1, 2))
    gmat_q = jnp.where((grp[:, None] == grp[None, :]) & (grp[:, None] < 2),
                       jnp.where(grp[:, None] == 0, 1.0 / MLA_NOPE, 1.0 / MLA_ROPE), 0.0)
    hid = jnp.arange(256) // RWKV_HEAD
    w_out = f("even_w_out").astype(BF16)
    zpad = jnp.zeros((RWKV_HEAD, RWKV_DIM), F32)
    return {
        "w_in": [w_mla, w_rw],
        "g_qlat": f("mla_g_qlat").reshape(1, -1), "g_kvlat": f("mla_g_kvlat").reshape(1, -1),
        "w_uq": w_uq.astype(BF16),
        "g_qhead": jnp.concatenate([f("mla_g_qn"), f("mla_g_qr"), jnp.zeros((HEAD_PAD - per_q,), F32)]).reshape(1, -1),
        "g_kr": _lane_row(f("mla_g_kr"), MLA_NOPE),
        "gmat_q": gmat_q.astype(BF16),
        "w_uk": w_uk.reshape(MLA_KV_RANK, -1).astype(BF16), "w_uv": w_uv.reshape(MLA_KV_RANK, -1).astype(BF16),
        "g_kn": _lane_row(f("mla_g_kn"), 0),
        "mu": f("rwkv_mu").reshape(1, -1), "w0": f("rwkv_w0").reshape(1, -1), "a0": f("rwkv_a0").reshape(1, -1),
        "w2": jnp.concatenate([f("rwkv_w2"), zpad], axis=0).astype(BF16),
        "a2": jnp.concatenate([zpad, f("rwkv_a2")], axis=0).astype(BF16),
        "g2": f("rwkv_g2").astype(BF16),
        "k_k": f("rwkv_k_k").reshape(1, -1), "k_a": f("rwkv_k_a").reshape(1, -1),
        "r_k": f("rwkv_r_k").reshape(1, -1),
        "lnx_g": f("rwkv_lnx_g").reshape(1, -1), "lnx_b": f("rwkv_lnx_b").reshape(1, -1),
        "gsum": (hid[:, None] == hid[None, :]).astype(BF16),
        "w_out1": w_out[:MLA_HEADS * MLA_V], "w_out2": w_out[MLA_HEADS * MLA_V:],
    }


def _prep_odd(W, i):
    f = lambda name: W[name][i]
    w_in = f("odd_w_in")
    c1 = POOL_DIM + GDN_QKV
    c2 = c1 + GDN_DIM
    w_out = f("odd_w_out").astype(BF16)
    return {
        "w_in": [w_in[:, :POOL_DIM].astype(BF16), w_in[:, POOL_DIM:c1].astype(BF16), w_in[:, c1:c2].astype(BF16),
                 _pad_cols(w_in[:, c2:], LANES).astype(BF16)],
        "pool_w": f("pool_w").astype(BF16), "pool_scale": f("pool_scale").reshape(1, -1),
        "conv_w": f("gdn_conv_w"),
        "a_log": _lane_row(f("gdn_a_log"), GDN_HEADS), "dt_bias": _lane_row(f("gdn_dt_bias"), GDN_HEADS),
        "o_g": f("gdn_o_g").reshape(1, -1),
        "sel": (jnp.arange(LANES)[None, :] == (jnp.arange(8)[:, None] + GDN_HEADS)).astype(BF16),
        "w_out1": w_out[:POOL_DIM], "w_out2": w_out[POOL_DIM:],
    }


def _rope_tabs(pos0, t):
    inv = 1.0 / (ROPE_THETA ** (jnp.arange(0, MLA_ROPE, 2, dtype=F32) / MLA_ROPE))
    ang = (pos0 + jnp.arange(t, dtype=jnp.int32)).astype(F32)[:, None] * inv[None, :]
    cos, sin = jnp.cos(ang), jnp.sin(ang)
    z16 = jnp.zeros_like(cos)
    one = jnp.ones((t, MLA_NOPE), F32)
    z64 = jnp.zeros((t, MLA_NOPE), F32)
    z32 = jnp.zeros((t, LANES - MLA_NOPE - MLA_ROPE), F32)
    return (jnp.concatenate([one, cos, cos, z32], axis=1),
            jnp.concatenate([z64, -sin, z16, z32], axis=1),
            jnp.concatenate([z64, z16, sin, z32], axis=1))


def _to_blockdiag(s):
    b = s.shape[0]
    s = s.reshape(b, RWKV_HEADS // 2, 2, RWKV_HEAD, RWKV_HEAD)
    z = jnp.zeros_like(s[:, :, 0])
    top = jnp.concatenate([s[:, :, 0], z], axis=-1)
    bot = jnp.concatenate([z, s[:, :, 1]], axis=-1)
    return jnp.concatenate([top, bot], axis=-2)


def _from_blockdiag(sb):
    b = sb.shape[0]
    s0 = sb[:, :, :RWKV_HEAD, :RWKV_HEAD]
    s1 = sb[:, :, RWKV_HEAD:, RWKV_HEAD:]
    return jnp.stack([s0, s1], axis=2).reshape(b, RWKV_HEADS, RWKV_HEAD, RWKV_HEAD)


def _trunk(x, mod, pos0, caches, W, prep):
    b, t, _ = x.shape
    ckv_c, kpe_c, shift_c, wkv_c, pool_c, conv_c, gdn_c, ffn_c = caches
    tabs = _rope_tabs(pos0, t)
    o_ckv, o_kpe, o_shift, o_wkv, o_pool, o_conv, o_gdn, o_ffn = [], [], [], [], [], [], [], []
    for layer in range(DEPTH):
        i = layer // 2
        wl = prep[layer]
        mod_l = mod[layer]
        if layer % 2 == 0:
            p_mla, rw = _inproj(x, mod_l, W["norm_mix_g"][layer].reshape(1, -1), wl["w_in"])
            q, ckv_new, kpe_new, kpe_blk = _mla_prep(p_mla, tabs, wl)
            if ckv_c is None:
                ckv_all, kblk_all, past = ckv_new, kpe_blk, 0
            else:
                past = ckv_c.shape[2]
                ckv_all = jnp.concatenate([ckv_c[i], ckv_new], axis=1)
                past_blk = jnp.pad(kpe_c[i], ((0, 0), (0, 0), (MLA_NOPE, LANES - MLA_NOPE - MLA_ROPE)))
                kblk_all = jnp.concatenate([past_blk, kpe_blk], axis=1)
            kk, vv = _mla_kv(ckv_all, kblk_all, wl)
            m1 = _attention(q, kk, vv, past)
            m2, sblk = _rwkv(rw, shift_c[i][:, None, :], _to_blockdiag(wkv_c[i]), wl)
            o_ckv.append(ckv_new)
            o_kpe.append(kpe_new)
            o_shift.append(rw[:, -1])
            o_wkv.append(_from_blockdiag(sblk))
        else:
            u, qkv, z, ba = _inproj(x, mod_l, W["norm_mix_g"][layer].reshape(1, -1), wl["w_in"])
            hist16 = jnp.pad(pool_c[i], ((0, 0), (1, 0), (0, 0)))
            m1 = _pool(u, hist16, pos0, wl)
            m2, gs = _gdn(qkv, z, ba, conv_c[i], gdn_c[i], wl)
            o_pool.append(u[:, t - POOL_HIST:])
            o_conv.append(qkv[:, t - (GDN_CONV - 1):])
            o_gdn.append(gs)
        x = _outproj(x, m1, m2, mod_l, wl["w_out1"], wl["w_out2"])
        x, fc = _ffn(x, mod_l, W["norm_ffn_g"][layer].reshape(1, -1), ffn_c[layer],
                     W["ffn_w_gate"][layer].astype(BF16), W["ffn_w_up"][layer].astype(BF16),
                     W["ffn_conv_w"][layer], W["ffn_w_down"][layer].astype(BF16))
        o_ffn.append(fc)
    st = jnp.stack
    return x, (st(o_ckv), st(o_kpe), st(o_shift), st(o_wkv), st(o_pool), st(o_conv), st(o_gdn), st(o_ffn))


def _zero_caches(b, dt):
    n_even, n_odd = (DEPTH + 1) // 2, DEPTH // 2
    return (None, None,
            jnp.zeros((n_even, b, RWKV_COLS), dt),
            jnp.zeros((n_even, b, RWKV_HEADS, RWKV_HEAD, RWKV_HEAD), dt),
            jnp.zeros((n_odd, b, POOL_HIST, POOL_DIM), dt),
            jnp.zeros((n_odd, b, GDN_CONV - 1, GDN_QKV), dt),
            jnp.zeros((n_odd, b, GDN_HEADS, GDN_DK, GDN_DV), dt),
            jnp.zeros((DEPTH, b, FFN_CONV - 1, D_FF), dt))


def _run(x_prompt, x_sample, caches_s, c_prompt, c_sample, W):
    bp = x_prompt.shape[0]
    mod = _ada(jnp.concatenate([c_prompt, c_sample], axis=0), W["ada_w"], W["ada_b"])
    mod = mod.reshape(DEPTH, -1, 6, D_MODEL)
    prep = [(_prep_even if l % 2 == 0 else _prep_odd)(W, l // 2) for l in range(DEPTH)]
    y_p, st_p = _trunk(x_prompt, mod[:, :bp], 0, _zero_caches(bp, x_prompt.dtype), W, prep)
    past_len = caches_s[0].shape[2]
    y_s, st_s = _trunk(x_sample, mod[:, bp:], past_len, caches_s, W, prep)
    return (y_p, y_s) + tuple(st_p) + tuple(st_s)


def kernel(x_prompt, x_sample, cache_mla_ckv, cache_mla_kpe, state_rwkv_shift, state_rwkv_wkv, state_pool,
           state_gdn_conv, state_gdn, state_ffn_conv, c_prompt, c_sample, ada_w, ada_b, norm_mix_g, norm_ffn_g,
           even_w_in, mla_g_qlat, mla_g_kvlat, mla_w_uq, mla_w_ukv, mla_g_qn, mla_g_qr, mla_g_kn, mla_g_kr,
           rwkv_mu, rwkv_w0, rwkv_w2, rwkv_a0, rwkv_a2, rwkv_g2, rwkv_k_k, rwkv_k_a, rwkv_r_k, rwkv_lnx_g,
           rwkv_lnx_b, even_w_out, odd_w_in, pool_w, pool_scale, gdn_conv_w, gdn_a_log, gdn_dt_bias, gdn_o_g,
           odd_w_out, ffn_w_gate, ffn_w_up, ffn_conv_w, ffn_w_down):
    W = {
        "ada_w": ada_w, "ada_b": ada_b, "norm_mix_g": norm_mix_g, "norm_ffn_g": norm_ffn_g,
        "even_w_in": even_w_in, "mla_g_qlat": mla_g_qlat, "mla_g_kvlat": mla_g_kvlat,
        "mla_w_uq": mla_w_uq, "mla_w_ukv": mla_w_ukv, "mla_g_qn": mla_g_qn, "mla_g_qr": mla_g_qr,
        "mla_g_kn": mla_g_kn, "mla_g_kr": mla_g_kr,
        "rwkv_mu": rwkv_mu, "rwkv_w0": rwkv_w0, "rwkv_w2": rwkv_w2, "rwkv_a0": rwkv_a0, "rwkv_a2": rwkv_a2,
        "rwkv_g2": rwkv_g2, "rwkv_k_k": rwkv_k_k, "rwkv_k_a": rwkv_k_a, "rwkv_r_k": rwkv_r_k,
        "rwkv_lnx_g": rwkv_lnx_g, "rwkv_lnx_b": rwkv_lnx_b, "even_w_out": even_w_out,
        "odd_w_in": odd_w_in, "pool_w": pool_w, "pool_scale": pool_scale, "gdn_conv_w": gdn_conv_w,
        "gdn_a_log": gdn_a_log, "gdn_dt_bias": gdn_dt_bias, "gdn_o_g": gdn_o_g, "odd_w_out": odd_w_out,
        "ffn_w_gate": ffn_w_gate, "ffn_w_up": ffn_w_up, "ffn_conv_w": ffn_conv_w, "ffn_w_down": ffn_w_down,
    }
    caches_s = (cache_mla_ckv, cache_mla_kpe, state_rwkv_shift, state_rwkv_wkv, state_pool, state_gdn_conv,
                state_gdn, state_ffn_conv)
    return _run(x_prompt, x_sample, caches_s, c_prompt, c_sample, W)
```

```python
import functools
import math

import jax
import jax.numpy as jnp
from jax import lax
from jax.experimental import pallas as pl
from jax.experimental.pallas import tpu as pltpu

F32 = jnp.float32
BF16 = jnp.bfloat16

D_MODEL = 1024
DEPTH = 4
CHUNK = 64
EPS = 1e-6
L2_EPS = 1e-6
NEG_INF = -1e30

MLA_HEADS = 8
MLA_NOPE = 64
MLA_ROPE = 32
MLA_V = 64
MLA_Q_RANK = 256
MLA_KV_RANK = 128
MLA_SCALE = (MLA_NOPE + MLA_ROPE) ** -0.5
ROPE_THETA = 10000.0

RWKV_HEADS = 8
RWKV_HEAD = 64
RWKV_DIM = 512
RWKV_COLS = 1792
RWKV_LN_EPS = 64e-5

POOL_DIM = 512
POOL_WINDOWS = (2, 4, 8, 16)
POOL_HIST = 15

GDN_HEADS = 4
GDN_DK = 128
GDN_DV = 128
GDN_CONV = 4
GDN_QKV = 1536
GDN_DIM = 512

D_FF = 2816
FFN_CONV = 3

LANES = 128
HEAD_PAD = 128
VMEM_LIMIT = 56 * 1024 * 1024


def _dot(a, b):
    return jnp.dot(a, b, preferred_element_type=F32)


def _dot_nt(a, b):
    return lax.dot_general(a, b, (((1,), (1,)), ((), ())), preferred_element_type=F32)


def _dot_tn(a, b):
    return lax.dot_general(a, b, (((0,), (0,)), ((), ())), preferred_element_type=F32)


def _split(x, n):
    parts = []
    r = x
    for i in range(n):
        p = r.astype(BF16)
        parts.append(p)
        if i + 1 < n:
            r = r - p.astype(F32)
    return parts


def _sum_left(mat01, x, n):
    out = None
    for p in _split(x, n):
        d = _dot(mat01, p)
        out = d if out is None else out + d
    return out


def _sum_right(x, mat01, n):
    out = None
    for p in _split(x, n):
        d = _dot(p, mat01)
        out = d if out is None else out + d
    return out


def _sigmoid(x):
    return 1.0 / (1.0 + jnp.exp(-x))


def _silu(x):
    return x * _sigmoid(x)


def _softplus(x):
    return jnp.maximum(x, 0.0) + jnp.log(1.0 + jnp.exp(-jnp.abs(x)))


def _rms(x, gain, eps=EPS):
    return x * lax.rsqrt(jnp.mean(x * x, axis=-1, keepdims=True) + eps) * gain


def _modulate(x, gain, shift, scale):
    return _rms(x, gain) * (1.0 + scale) + shift


def _solve_unit_lower(ns, rhss, steps):
    size = ns[0].shape[0]
    ri = lax.broadcasted_iota(jnp.int32, (size, size), 0)
    ci = lax.broadcasted_iota(jnp.int32, (size, size), 1)
    eye = jnp.where(ri == ci, 1.0, 0.0)
    xs = [eye + n for n in ns]
    pbs = [n.astype(BF16) for n in ns]
    pbs = [_dot(p, p).astype(BF16) for p in pbs]
    for _ in range(steps - 1):
        xs = [x + _dot(x.astype(BF16), p) for x, p in zip(xs, pbs)]
        pbs = [_dot(p, p).astype(BF16) for p in pbs]
    xbs = [(x + _dot(x.astype(BF16), p)).astype(BF16) for x, p in zip(xs, pbs)]
    us = [_dot(xb, r.astype(BF16)) for xb, r in zip(xbs, rhss)]
    nsp = [_split(n, 2) for n in ns]
    usp = [_split(u, 2) for u in us]
    nus = [_dot(nh, uh) for (nh, _), (uh, _) in zip(nsp, usp)]
    nus = [nu + _dot(nh, ul) for nu, (nh, _), (_, ul) in zip(nus, nsp, usp)]
    nus = [nu + _dot(nl, uh) for nu, (_, nl), (uh, _) in zip(nus, nsp, usp)]
    return [u + _dot(xb, (r - u + nu).astype(BF16)) for u, xb, r, nu in zip(us, xbs, rhss, nus)]


def _const_spec(shape):
    nd = len(shape)
    return pl.BlockSpec(shape, lambda *_: (0,) * nd, pipeline_mode=pl.Buffered(1))


def _params(n_axes):
    return pltpu.CompilerParams(dimension_semantics=("arbitrary",) * n_axes, vmem_limit_bytes=VMEM_LIMIT)


def _time_tile(t, pref):
    return pref if t % pref == 0 else t


def _ada_kernel(c_ref, w_ref, b_ref, o_ref):
    s = _silu(c_ref[...]).astype(BF16)
    o_ref[0] = _dot(s, w_ref[0].astype(BF16)) + b_ref[0]


def _ada(c_all, ada_w, ada_b):
    n = c_all.shape[0]
    tn = 1536
    return pl.pallas_call(
        _ada_kernel,
        out_shape=jax.ShapeDtypeStruct((DEPTH, n, 6 * D_MODEL), F32),
        grid=(DEPTH, 6 * D_MODEL // tn),
        in_specs=[
            pl.BlockSpec((n, D_MODEL), lambda l, j: (0, 0)),
            pl.BlockSpec((1, D_MODEL, tn), lambda l, j: (l, 0, j)),
            pl.BlockSpec((1, 1, tn), lambda l, j: (l, 0, j)),
        ],
        out_specs=pl.BlockSpec((1, n, tn), lambda l, j: (l, 0, j)),
        compiler_params=_params(2),
        name="ada_mod",
    )(c_all, ada_w, ada_b.reshape(DEPTH, 1, 6 * D_MODEL))


def _inproj_kernel(x_ref, mod_ref, g_ref, *refs, nseg):
    m = mod_ref[0]
    h = _modulate(x_ref[0], g_ref[...], m[0:1], m[1:2]).astype(BF16)
    for w_ref, o_ref in zip(refs[:nseg], refs[nseg:]):
        o_ref[0] = _dot(h, w_ref[...])


def _inproj(x, mod_l, gain, w_segs):
    b, t, d = x.shape
    tm = _time_tile(t, 512)
    nseg = len(w_segs)
    return pl.pallas_call(
        functools.partial(_inproj_kernel, nseg=nseg),
        out_shape=[jax.ShapeDtypeStruct((b, t, w.shape[1]), F32) for w in w_segs],
        grid=(b, t // tm),
        in_specs=[
            pl.BlockSpec((1, tm, d), lambda i, j: (i, j, 0)),
            pl.BlockSpec((1, 6, d), lambda i, j: (i, 0, 0)),
            _const_spec((1, d)),
        ] + [_const_spec(w.shape) for w in w_segs],
        out_specs=[pl.BlockSpec((1, tm, w.shape[1]), lambda i, j: (i, j, 0)) for w in w_segs],
        compiler_params=_params(2),
        name="in_proj",
    )(x, mod_l, gain, *w_segs)


def _rope(blk, c, sm, sp):
    return blk * c + pltpu.roll(blk, LANES - 16, 1) * sm + pltpu.roll(blk, 16, 1) * sp


def _mla_prep_kernel(p_ref, c_ref, sm_ref, sp_ref, gq_ref, gkv_ref, wuq_ref, ghead_ref, gkr_ref, gmat_ref,
                     q_ref, ckv_ref, kpe_ref, kblk_ref):
    p = p_ref[0]
    c, sm, sp = c_ref[...], sm_ref[...], sp_ref[...]
    qn = _rms(p[:, :MLA_Q_RANK], gq_ref[...]).astype(BF16)
    q = _dot(qn, wuq_ref[...])
    gmat = gmat_ref[...]
    for h in range(MLA_HEADS):
        hb = q[:, h * HEAD_PAD:(h + 1) * HEAD_PAD]
        ms = _sum_right(hb * hb, gmat, 2)
        hb = hb * lax.rsqrt(ms + EPS) * ghead_ref[...]
        q_ref[0, h] = _rope(hb, c, sm, sp).astype(BF16)
    ckv_ref[0] = _rms(p[:, MLA_Q_RANK:MLA_Q_RANK + MLA_KV_RANK], gkv_ref[...])
    kr = p[:, MLA_Q_RANK + MLA_KV_RANK:]
    ms = jnp.sum(kr * kr, axis=-1, keepdims=True) * (1.0 / MLA_ROPE)
    kb = _rope(kr * lax.rsqrt(ms + EPS) * gkr_ref[...], c, sm, sp)
    kblk_ref[0] = kb
    kpe_ref[0] = kb[:, MLA_NOPE:MLA_NOPE + MLA_ROPE]


def _mla_prep(p_mla, tabs, wl):
    b, t, _ = p_mla.shape
    tm = _time_tile(t, 512)
    tab_spec = pl.BlockSpec((tm, LANES), lambda i, j: (j, 0))
    return pl.pallas_call(
        _mla_prep_kernel,
        out_shape=[
            jax.ShapeDtypeStruct((b, MLA_HEADS, t, HEAD_PAD), BF16),
            jax.ShapeDtypeStruct((b, t, MLA_KV_RANK), F32),
            jax.ShapeDtypeStruct((b, t, MLA_ROPE), F32),
            jax.ShapeDtypeStruct((b, t, LANES), F32),
        ],
        grid=(b, t // tm),
        in_specs=[
            pl.BlockSpec((1, tm, 512), lambda i, j: (i, j, 0)),
            tab_spec, tab_spec, tab_spec,
            _const_spec((1, MLA_Q_RANK)), _const_spec((1, MLA_KV_RANK)),
            _const_spec((MLA_Q_RANK, MLA_HEADS * HEAD_PAD)),
            _const_spec((1, LANES)), _const_spec((1, LANES)), _const_spec((LANES, LANES)),
        ],
        out_specs=[
            pl.BlockSpec((1, MLA_HEADS, tm, HEAD_PAD), lambda i, j: (i, 0, j, 0)),
            pl.BlockSpec((1, tm, MLA_KV_RANK), lambda i, j: (i, j, 0)),
            pl.BlockSpec((1, tm, MLA_ROPE), lambda i, j: (i, j, 0)),
            pl.BlockSpec((1, tm, LANES), lambda i, j: (i, j, 0)),
        ],
        compiler_params=_params(2),
        name="mla_prep",
    )(p_mla, tabs[0], tabs[1], tabs[2], wl["g_qlat"], wl["g_kvlat"], wl["w_uq"], wl["g_qhead"], wl["g_kr"],
      wl["gmat_q"])


def _mla_kv_kernel(ckv_ref, kpe_ref, wuk_ref, wuv_ref, gkn_ref, k_ref, v_ref):
    cb = ckv_ref[0].astype(BF16)
    kf = _dot(cb, wuk_ref[...])
    vf = _dot(cb, wuv_ref[...])
    kpe = kpe_ref[0]
    for h in range(MLA_HEADS):
        kb = kf[:, h * HEAD_PAD:(h + 1) * HEAD_PAD]
        ms = jnp.sum(kb * kb, axis=-1, keepdims=True) * (1.0 / MLA_NOPE)
        k_ref[0, h] = (kb * lax.rsqrt(ms + EPS) * gkn_ref[...] + kpe).astype(BF16)
        v_ref[0, h] = vf[:, h * HEAD_PAD:(h + 1) * HEAD_PAD].astype(BF16)


def _mla_kv(ckv_all, kpe_blk, wl):
    b, l, _ = ckv_all.shape
    tl = _time_tile(l, 512)
    return pl.pallas_call(
        _mla_kv_kernel,
        out_shape=[jax.ShapeDtypeStruct((b, MLA_HEADS, l, HEAD_PAD), BF16)] * 2,
        grid=(b, l // tl),
        in_specs=[
            pl.BlockSpec((1, tl, MLA_KV_RANK), lambda i, j: (i, j, 0)),
            pl.BlockSpec((1, tl, LANES), lambda i, j: (i, j, 0)),
            _const_spec((MLA_KV_RANK, MLA_HEADS * HEAD_PAD)),
            _const_spec((MLA_KV_RANK, MLA_HEADS * HEAD_PAD)),
            _const_spec((1, LANES)),
        ],
        out_specs=[pl.BlockSpec((1, MLA_HEADS, tl, HEAD_PAD), lambda i, j: (i, 0, j, 0))] * 2,
        compiler_params=_params(2),
        name="mla_kv",
    )(ckv_all, kpe_blk, wl["w_uk"], wl["w_uv"], wl["g_kn"])


def _attn_kernel(q_ref, k_ref, v_ref, o_ref, *, tq, tk, past_len):
    d0 = pl.multiple_of(past_len + pl.program_id(1) * tq, tq)
    nprefix = d0 // tk
    qc = lax.broadcasted_iota(jnp.int32, (tq, tq), 0) // CHUNK
    kc = lax.broadcasted_iota(jnp.int32, (tq, tq), 1) // CHUNK
    mask = kc <= qc
    heads = range(MLA_HEADS)
    ss = [_dot_nt(q_ref[0, h], k_ref[0, h, pl.ds(d0, tq), :]) for h in heads]
    ss = [jnp.where(mask, s * MLA_SCALE, NEG_INF) for s in ss]
    ms = [jnp.max(s, axis=-1, keepdims=True) for s in ss]
    ps = [jnp.exp(s - m) for s, m in zip(ss, ms)]
    ls = [jnp.sum(p, axis=-1, keepdims=True) for p in ps]
    accs = [_dot(p.astype(BF16), v_ref[0, h, pl.ds(d0, tq), :]) for p, h in zip(ps, heads)]

    def body(j, carry):
        ms, ls, accs = carry
        r0 = pl.multiple_of(j * tk, tk)
        ss = [_dot_nt(q_ref[0, h], k_ref[0, h, pl.ds(r0, tk), :]) * MLA_SCALE for h in heads]
        mns = [jnp.maximum(m, jnp.max(s, axis=-1, keepdims=True)) for m, s in zip(ms, ss)]
        alphas = [jnp.exp(m - mn) for m, mn in zip(ms, mns)]
        ps = [jnp.exp(s - mn) for s, mn in zip(ss, mns)]
        ls = [a * l + jnp.sum(p, axis=-1, keepdims=True) for a, l, p in zip(alphas, ls, ps)]
        pvs = [_dot(p.astype(BF16), v_ref[0, h, pl.ds(r0, tk), :]) for p, h in zip(ps, heads)]
        accs = [a * acc + pv for a, acc, pv in zip(alphas, accs, pvs)]
        return tuple(mns), tuple(ls), tuple(accs)

    ms, ls, accs = lax.fori_loop(0, nprefix, body, (tuple(ms), tuple(ls), tuple(accs)))
    outs = [acc / l for acc, l in zip(accs, ls)]
    for j in range(MLA_HEADS // 2):
        o_ref[0, :, j * LANES:(j + 1) * LANES] = outs[2 * j] + outs[2 * j + 1]


def _attention(q, k, v, past_len):
    b, hh, t, _ = q.shape
    l = k.shape[2]
    tq = _time_tile(t, 256)
    tk = 256
    assert past_len % tk == 0 and (tq % tk == 0 or past_len + tq == l) and l == past_len + t
    return pl.pallas_call(
        functools.partial(_attn_kernel, tq=tq, tk=tk, past_len=past_len),
        out_shape=jax.ShapeDtypeStruct((b, t, MLA_HEADS * MLA_V), F32),
        grid=(b, t // tq),
        in_specs=[
            pl.BlockSpec((1, hh, tq, HEAD_PAD), lambda i, j: (i, 0, j, 0)),
            pl.BlockSpec((1, hh, l, HEAD_PAD), lambda i, j: (i, 0, 0, 0)),
            pl.BlockSpec((1, hh, l, HEAD_PAD), lambda i, j: (i, 0, 0, 0)),
        ],
        out_specs=pl.BlockSpec((1, tq, MLA_HEADS * MLA_V), lambda i, j: (i, j, 0)),
        compiler_params=_params(2),
        name="mla_attn",
    )(q, k, v)


def _rwkv_kernel(rw_ref, sh_ref, s0_ref, mu_ref, w0_ref, w2_ref, a0_ref, a2_ref, g2_ref, kkw_ref, ka_ref,
                 rk_ref, lng_ref, lnb_ref, gsum_ref, y_ref, s_ref, xs_scr, st_scr, *, tt, cc):
    t = pl.program_id(1)

    @pl.when(t == 0)
    def _():
        xs_scr[7:8, :] = sh_ref[0]
        st_scr[...] = s0_ref[0]

    rw = rw_ref[0]
    xs_scr[8:8 + tt, :] = rw
    prev = xs_scr[7:7 + tt, :]
    xm = rw + (prev - rw) * mu_ref[...]
    xs_scr[7:8, :] = rw[tt - 1:tt, :]

    r = xm[:, 0:RWKV_DIM]
    k = xm[:, RWKV_DIM:2 * RWKV_DIM]
    v = xm[:, 2 * RWKV_DIM:3 * RWKV_DIM]
    lw = xm[:, 3 * RWKV_DIM:3 * RWKV_DIM + LANES]
    dg = xm[:, 3 * RWKV_DIM + LANES:]
    zw = w0_ref[...] + _dot(jnp.tanh(lw).astype(BF16), w2_ref[...])
    ld = -jnp.exp(-_softplus(-zw) - 0.5)
    a = _sigmoid(a0_ref[...] + _dot(lw.astype(BF16), a2_ref[...]))
    g = _dot(_sigmoid(dg).astype(BF16), g2_ref[...])

    gmat = gsum_ref[...]

    def gsum(x):
        return jnp.concatenate([_sum_right(x[:, :256], gmat, 2), _sum_right(x[:, 256:], gmat, 2)], axis=-1)

    kkr = k * kkw_ref[...]
    kk = kkr * lax.rsqrt(gsum(kkr * kkr) + L2_EPS)
    k2 = k * (1.0 + (a - 1.0) * ka_ref[...])
    kka = kk * a
    bonus = gsum(r * k2 * rk_ref[...]) * v

    c2 = 2 * cc
    nch = tt // cc
    npair = RWKV_HEADS // 2
    ri = lax.broadcasted_iota(jnp.int32, (tt, tt), 0)
    ci = lax.broadcasted_iota(jnp.int32, (tt, tt), 1)
    tri = jnp.where(ci <= ri, jnp.where(ci // cc == ri // cc, 1.0, 0.0), 0.0).astype(BF16)
    rl = lax.broadcasted_iota(jnp.int32, (c2, 2 * c2), 0) % cc
    cl = lax.broadcasted_iota(jnp.int32, (c2, 2 * c2), 1) % cc
    strict = cl < rl
    incl = cl <= rl
    low = lax.broadcasted_iota(jnp.int32, (1, LANES), 1) < RWKV_HEAD

    cs = _sum_left(tri, ld, 3)
    tots = [cs[(c + 1) * cc - 1:(c + 1) * cc, :] for c in range(nch)]
    tot = jnp.concatenate([jnp.broadcast_to(tc, (cc, RWKV_DIM)) for tc in tots], axis=0) if nch > 1 else tots[0]
    pinv = jnp.exp(-cs)
    pend = jnp.exp(tot - cs)
    at = -(kk * jnp.exp(cs - ld))
    rt = r * jnp.exp(cs)
    bt = kka * pinv
    kt = k2 * pinv
    bh = kka * pend
    kh = k2 * pend

    inst = [(c, j) for c in range(nch) for j in range(npair)]

    def blk(x, c, j):
        xp = x[c * cc:(c + 1) * cc, j * LANES:(j + 1) * LANES]
        return jnp.concatenate([jnp.where(low, xp, 0.0), jnp.where(low, 0.0, xp)], axis=0)

    a_b = [blk(at, c, j) for c, j in inst]
    r_b = [blk(rt, c, j) for c, j in inst]
    v_b = [blk(v, c, j) for c, j in inst]
    bk_b = [jnp.concatenate([blk(bh, c, j), blk(kh, c, j)], axis=0).astype(BF16) for c, j in inst]
    lhs = [jnp.concatenate([a, rr], axis=0).astype(BF16) for a, rr in zip(a_b, r_b)]
    rhs = [jnp.concatenate([blk(bt, c, j), blk(kt, c, j)], axis=0).astype(BF16) for c, j in inst]
    mm = [_dot_nt(l, rr) for l, rr in zip(lhs, rhs)]
    mt = [jnp.where(strict, m[:c2], 0.0) for m in mm]
    mb = [jnp.where(incl, m[c2:], 0.0).astype(BF16) for m in mm]
    akv = [_dot(m.astype(BF16), jnp.concatenate([jnp.zeros_like(vb), vb], axis=0).astype(BF16))
           for m, vb in zip(mt, v_b)]
    zs = _solve_unit_lower([m[:, :c2] for m in mt], [jnp.concatenate([a, w], axis=1) for a, w in zip(a_b, akv)],
                           int(math.log2(cc)) - 1)
    ahat = [z[:, :LANES].astype(BF16) for z in zs]
    wv = [jnp.concatenate([z[:, LANES:], vb], axis=0).astype(BF16) for z, vb in zip(zs, v_b)]
    rhat = [(rr + _dot(m[:, :c2], ah)).astype(BF16) for rr, m, ah in zip(r_b, mb, ahat)]
    yc = [_dot(m, w) for m, w in zip(mb, wv)]
    phi = [_dot_tn(ah, bk[:c2]).astype(BF16) for ah, bk in zip(ahat, bk_b)]
    gam = [_dot_tn(w, bk) for w, bk in zip(wv, bk_b)]

    sb = [st_scr[j] for j in range(npair)]
    ys = []
    for c in range(nch):
        ptot = jnp.exp(tots[c])
        sbb = [s.astype(BF16) for s in sb]
        yb = [_dot_nt(rhat[c * npair + j], sbb[j]) + yc[c * npair + j] for j in range(npair)]
        sb = [sb[j] * ptot[:, j * LANES:(j + 1) * LANES] + _dot(sbb[j], phi[c * npair + j]) + gam[c * npair + j]
              for j in range(npair)]
        ys.append(jnp.concatenate([yy[:cc] + yy[cc:] for yy in yb], axis=-1))
    for j in range(npair):
        st_scr[j] = sb[j]
    y = jnp.concatenate(ys, axis=0) if nch > 1 else ys[0]

    mu = gsum(y) * (1.0 / RWKV_HEAD)
    dy = y - mu
    var = gsum(dy * dy) * (1.0 / RWKV_HEAD)
    yn = dy * lax.rsqrt(var + RWKV_LN_EPS) * lng_ref[...] + lnb_ref[...]
    y_ref[0] = (yn + bonus) * g
    s_ref[0] = st_scr[...]


def _rwkv(rw, shift, sblk, wl):
    b, t, _ = rw.shape
    cc = min(t, CHUNK)
    tt = _time_tile(t, 256)
    row = lambda n: _const_spec((1, n))
    return pl.pallas_call(
        functools.partial(_rwkv_kernel, tt=tt, cc=cc),
        out_shape=[
            jax.ShapeDtypeStruct((b, t, RWKV_DIM), F32),
            jax.ShapeDtypeStruct((b, RWKV_HEADS // 2, LANES, LANES), F32),
        ],
        grid=(b, t // tt),
        in_specs=[
            pl.BlockSpec((1, tt, RWKV_COLS), lambda i, j: (i, j, 0)),
            pl.BlockSpec((1, 1, RWKV_COLS), lambda i, j: (i, 0, 0)),
            pl.BlockSpec((1, RWKV_HEADS // 2, LANES, LANES), lambda i, j: (i, 0, 0, 0)),
            row(RWKV_COLS), row(RWKV_DIM), _const_spec((LANES, RWKV_DIM)), row(RWKV_DIM),
            _const_spec((LANES, RWKV_DIM)), _const_spec((LANES, RWKV_DIM)),
            row(RWKV_DIM), row(RWKV_DIM), row(RWKV_DIM), row(RWKV_DIM), row(RWKV_DIM),
            _const_spec((256, 256)),
        ],
        out_specs=[
            pl.BlockSpec((1, tt, RWKV_DIM), lambda i, j: (i, j, 0)),
            pl.BlockSpec((1, RWKV_HEADS // 2, LANES, LANES), lambda i, j: (i, 0, 0, 0)),
        ],
        scratch_shapes=[
            pltpu.VMEM((tt + 8, RWKV_COLS), F32),
            pltpu.VMEM((RWKV_HEADS // 2, LANES, LANES), F32),
        ],
        compiler_params=_params(2),
        name="rwkv7",
    )(rw, shift, sblk, wl["mu"], wl["w0"], wl["w2"], wl["a0"], wl["a2"], wl["g2"], wl["k_k"], wl["k_a"],
      wl["r_k"], wl["lnx_g"], wl["lnx_b"], wl["gsum"])


def _pool_kernel(u_ref, hist_ref, pw_ref, ps_ref, o_ref, scr, *, tt, pos0):
    t = pl.program_id(1)

    @pl.when(t == 0)
    def _():
        scr[0:16, :] = hist_ref[0]

    u = u_ref[0]
    scr[16:16 + tt, :] = u
    pos = pos0 + t * tt + lax.broadcasted_iota(jnp.int32, (tt, 1), 0)
    for gi, w in enumerate(POOL_WINDOWS):
        ls = slice(gi * LANES, (gi + 1) * LANES)
        ug = u[:, ls]
        s = ug
        for d in range(1, w):
            s = s + scr[16 - d:16 - d + tt, ls]
        cnt = jnp.minimum(w, pos + 1).astype(F32)
        diff = s / cnt - ug
        o_ref[0, :, ls] = _dot(diff.astype(BF16), pw_ref[gi]) * ps_ref[:, ls]
    scr[0:16, :] = scr[tt:tt + 16, :]


def _pool(u, hist16, pos0, wl):
    b, t, _ = u.shape
    tt = _time_tile(t, 256)
    return pl.pallas_call(
        functools.partial(_pool_kernel, tt=tt, pos0=pos0),
        out_shape=jax.ShapeDtypeStruct((b, t, POOL_DIM), F32),
        grid=(b, t // tt),
        in_specs=[
            pl.BlockSpec((1, tt, POOL_DIM), lambda i, j: (i, j, 0)),
            pl.BlockSpec((1, 16, POOL_DIM), lambda i, j: (i, 0, 0)),
            _const_spec((len(POOL_WINDOWS), LANES, LANES)),
            _const_spec((1, POOL_DIM)),
        ],
        out_specs=pl.BlockSpec((1, tt, POOL_DIM), lambda i, j: (i, j, 0)),
        scratch_shapes=[pltpu.VMEM((tt + 16, POOL_DIM), F32)],
        compiler_params=_params(2),
        name="pool_mix",
    )(u, hist16, wl["pool_w"], wl["pool_scale"])


def _gdn_kernel(qkv_ref, z_ref, ba_ref, hist_ref, s0_ref, cw_ref, alog_ref, dtb_ref, og_ref, sel_ref,
                o_ref, s_ref, x_scr, st_scr, *, tt, cc):
    t = pl.program_id(1)

    @pl.when(t == 0)
    def _():
        x_scr[5:8, :] = hist_ref[0]
        st_scr[...] = s0_ref[0]

    x = qkv_ref[0]
    x_scr[8:8 + tt, :] = x
    cw = cw_ref[...]
    xc = cw[3:4] * x + cw[2:3] * x_scr[7:7 + tt, :] + cw[1:2] * x_scr[6:6 + tt, :] + cw[0:1] * x_scr[5:5 + tt, :]
    x_scr[5:8, :] = x[tt - 3:tt, :]
    xc = _silu(xc)
    ba = ba_ref[0]
    beta_all = _sigmoid(ba)
    g_all = -jnp.exp(alog_ref[...]) * _softplus(ba + dtb_ref[...])

    nch = tt // cc
    ri = lax.broadcasted_iota(jnp.int32, (cc, cc), 0)
    ci = lax.broadcasted_iota(jnp.int32, (cc, cc), 1)
    incl = ci <= ri
    strict = ci < ri
    rt_ = lax.broadcasted_iota(jnp.int32, (tt, tt), 0)
    ct_ = lax.broadcasted_iota(jnp.int32, (tt, tt), 1)
    tri = jnp.where(ct_ <= rt_, jnp.where(ct_ // cc == rt_ // cc, 1.0, 0.0), 0.0).astype(BF16)
    sel = sel_ref[...]
    zz = z_ref[0]

    gcs = _sum_left(tri, g_all, 3)
    grows = []
    for c in range(nch):
        parts = _split(gcs[c * cc:(c + 1) * cc], 3)
        grows.append(_dot_nt(sel, parts[0]) + _dot_nt(sel, parts[1]) + _dot_nt(sel, parts[2]))

    qn, kn = [], []
    for h in range(GDN_HEADS):
        qh = xc[:, h * GDN_DK:(h + 1) * GDN_DK]
        kh = xc[:, (GDN_HEADS + h) * GDN_DK:(GDN_HEADS + h + 1) * GDN_DK]
        qn.append(qh * lax.rsqrt(jnp.sum(qh * qh, axis=-1, keepdims=True) + L2_EPS) * (GDN_DK ** -0.5))
        kn.append(kh * lax.rsqrt(jnp.sum(kh * kh, axis=-1, keepdims=True) + L2_EPS))

    inst = [(c, h) for c in range(nch) for h in range(GDN_HEADS)]
    v0 = 2 * GDN_HEADS * GDN_DK

    qs = [qn[h][c * cc:(c + 1) * cc] for c, h in inst]
    ks = [kn[h][c * cc:(c + 1) * cc] for c, h in inst]
    vs = [xc[c * cc:(c + 1) * cc, v0 + h * GDN_DV:v0 + (h + 1) * GDN_DV] for c, h in inst]
    betas = [beta_all[c * cc:(c + 1) * cc, h:h + 1] for c, h in inst]
    gcols = [gcs[c * cc:(c + 1) * cc, GDN_HEADS + h:GDN_HEADS + h + 1] for c, h in inst]
    glasts = [g[cc - 1:cc, :] for g in gcols]
    egs = [jnp.exp(g) for g in gcols]
    decs = [jnp.where(incl, jnp.exp(jnp.where(incl, g - grows[c][h:h + 1, :], 0.0)), 0.0)
            for g, (c, h) in zip(gcols, inst)]
    grams = [_dot_nt(jnp.concatenate([k_, q_], axis=0).astype(BF16), k_.astype(BF16)) for k_, q_ in zip(ks, qs)]
    n_mats = [-(b_ * gm[:cc] * jnp.where(strict, d_, 0.0)) for b_, gm, d_ in zip(betas, grams, decs)]
    qks = [(gm[cc:] * d_).astype(BF16) for gm, d_ in zip(grams, decs)]
    zs = _solve_unit_lower(n_mats, [jnp.concatenate([b_ * e_ * k_, b_ * v_], axis=1)
                                    for b_, e_, k_, v_ in zip(betas, egs, ks, vs)], int(math.log2(cc)) - 1)
    zb = [z.astype(BF16) for z in zs]
    qkz = [_dot(qk, z) for qk, z in zip(qks, zb)]
    qhat = [(e_ * q_ - t_[:, :GDN_DK]).astype(BF16) for e_, q_, t_ in zip(egs, qs, qkz)]
    pg = [_dot_tn((k_ * jnp.exp(gl - g)).astype(BF16), z) for k_, gl, g, z in zip(ks, glasts, gcols, zb)]

    sts = [st_scr[h] for h in range(GDN_HEADS)]
    os_ = []
    for c in range(nch):
        i0 = c * GDN_HEADS
        stb = [s_.astype(BF16) for s_ in sts]
        oo = [_dot(qhat[i0 + h], stb[h]) + qkz[i0 + h][:, GDN_DK:] for h in range(GDN_HEADS)]
        sts = [jnp.exp(glasts[i0 + h]) * sts[h] - _dot(pg[i0 + h][:, :GDN_DK].astype(BF16), stb[h])
               + pg[i0 + h][:, GDN_DK:] for h in range(GDN_HEADS)]
        os_.append(jnp.concatenate(
            [_rms(oo[h], og_ref[...]) * _silu(zz[c * cc:(c + 1) * cc, h * GDN_DV:(h + 1) * GDN_DV])
             for h in range(GDN_HEADS)], axis=-1))
    for h in range(GDN_HEADS):
        st_scr[h] = sts[h]
    o_ref[0] = jnp.concatenate(os_, axis=0) if nch > 1 else os_[0]
    s_ref[0] = st_scr[...]


def _gdn(qkv, z, ba, conv_hist, state, wl):
    b, t, _ = qkv.shape
    cc = min(t, CHUNK)
    tt = _time_tile(t, 256)
    return pl.pallas_call(
        functools.partial(_gdn_kernel, tt=tt, cc=cc),
        out_shape=[
            jax.ShapeDtypeStruct((b, t, GDN_DIM), F32),
            jax.ShapeDtypeStruct((b, GDN_HEADS, GDN_DK, GDN_DV), F32),
        ],
        grid=(b, t // tt),
        in_specs=[
            pl.BlockSpec((1, tt, GDN_QKV), lambda i, j: (i, j, 0)),
            pl.BlockSpec((1, tt, GDN_DIM), lambda i, j: (i, j, 0)),
            pl.BlockSpec((1, tt, LANES), lambda i, j: (i, j, 0)),
            pl.BlockSpec((1, GDN_CONV - 1, GDN_QKV), lambda i, j: (i, 0, 0)),
            pl.BlockSpec((1, GDN_HEADS, GDN_DK, GDN_DV), lambda i, j: (i, 0, 0, 0)),
            _const_spec((GDN_CONV, GDN_QKV)),
            _const_spec((1, LANES)), _const_spec((1, LANES)), _const_spec((1, GDN_DV)),
            _const_spec((8, LANES)),
        ],
        out_specs=[
            pl.BlockSpec((1, tt, GDN_DIM), lambda i, j: (i, j, 0)),
            pl.BlockSpec((1, GDN_HEADS, GDN_DK, GDN_DV), lambda i, j: (i, 0, 0, 0)),
        ],
        scratch_shapes=[
            pltpu.VMEM((tt + 8, GDN_QKV), F32),
            pltpu.VMEM((GDN_HEADS, GDN_DK, GDN_DV), F32),
        ],
        compiler_params=_params(2),
        name="gdn",
    )(qkv, z, ba, conv_hist, state, wl["conv_w"], wl["a_log"], wl["dt_bias"], wl["o_g"], wl["sel"])


def _outproj_kernel(x_ref, m1_ref, m2_ref, mod_ref, w1_ref, w2_ref, o_ref):
    y = _dot(m1_ref[0].astype(BF16), w1_ref[...]) + _dot(m2_ref[0].astype(BF16), w2_ref[...])
    o_ref[0] = x_ref[0] + mod_ref[0][2:3] * y


def _outproj(x, m1, m2, mod_l, w1, w2):
    b, t, d = x.shape
    tm = _time_tile(t, 512)
    half = m1.shape[2]
    return pl.pallas_call(
        _outproj_kernel,
        out_shape=jax.ShapeDtypeStruct((b, t, d), F32),
        grid=(b, t // tm),
        in_specs=[
            pl.BlockSpec((1, tm, d), lambda i, j: (i, j, 0)),
            pl.BlockSpec((1, tm, half), lambda i, j: (i, j, 0)),
            pl.BlockSpec((1, tm, half), lambda i, j: (i, j, 0)),
            pl.BlockSpec((1, 6, d), lambda i, j: (i, 0, 0)),
            _const_spec((half, d)), _const_spec((half, d)),
        ],
        out_specs=pl.BlockSpec((1, tm, d), lambda i, j: (i, j, 0)),
        compiler_params=_params(2),
        name="out_proj",
    )(x, m1, m2, mod_l, w1, w2)


def _ffn_kernel(x_ref, mod_ref, g_ref, hist_ref, wg_ref, wu_ref, cw_ref, wd_ref, o_ref, hn_ref, a_scr, *, tm, nfc):
    t = pl.program_id(1)

    @pl.when(t == 0)
    def _():
        a_scr[6:8, :] = hist_ref[0]

    x = x_ref[0]
    m = mod_ref[0]
    h = _modulate(x, g_ref[...], m[3:4], m[4:5]).astype(BF16)
    fc = D_FF // nfc
    acc = None
    for c in range(nfc):
        cs = slice(c * fc, (c + 1) * fc)
        a = _dot(h, wg_ref[:, cs])
        a_scr[8:8 + tm, cs] = a
        ac = cw_ref[2:3, cs] * a + cw_ref[1:2, cs] * a_scr[7:7 + tm, cs] + cw_ref[0:1, cs] * a_scr[6:6 + tm, cs]
        a_scr[6:8, cs] = a[tm - 2:tm, :]
        act = (_silu(ac) * _dot(h, wu_ref[:, cs])).astype(BF16)
        d = _dot(act, wd_ref[cs, :])
        acc = d if acc is None else acc + d
    o_ref[0] = x + m[5:6] * acc
    hn_ref[0] = a_scr[6:8, :]


def _ffn(x, mod_l, gain, hist, wg, wu, cw, wd):
    b, t, d = x.shape
    tm = _time_tile(t, 256)
    return pl.pallas_call(
        functools.partial(_ffn_kernel, tm=tm, nfc=2),
        out_shape=[
            jax.ShapeDtypeStruct((b, t, d), F32),
            jax.ShapeDtypeStruct((b, FFN_CONV - 1, D_FF), F32),
        ],
        grid=(b, t // tm),
        in_specs=[
            pl.BlockSpec((1, tm, d), lambda i, j: (i, j, 0)),
            pl.BlockSpec((1, 6, d), lambda i, j: (i, 0, 0)),
            _const_spec((1, d)),
            pl.BlockSpec((1, FFN_CONV - 1, D_FF), lambda i, j: (i, 0, 0)),
            _const_spec((d, D_FF)), _const_spec((d, D_FF)), _const_spec((FFN_CONV, D_FF)), _const_spec((D_FF, d)),
        ],
        out_specs=[
            pl.BlockSpec((1, tm, d), lambda i, j: (i, j, 0)),
            pl.BlockSpec((1, FFN_CONV - 1, D_FF), lambda i, j: (i, 0, 0)),
        ],
        scratch_shapes=[pltpu.VMEM((tm + 8, D_FF), F32)],
        compiler_params=_params(2),
        name="conv_ffn",
    )(x, mod_l, gain, hist, wg, wu, cw, wd)


def _pad_cols(w, n):
    return jnp.pad(w, ((0, 0), (0, n - w.shape[1])))


def _lane_row(vec, offset):
    return jnp.pad(vec, (offset, LANES - offset - vec.shape[0])).reshape(1, LANES)


def _prep_even(W, i):
    f = lambda name: W[name][i]
    w_in = f("even_w_in")
    mla_cols = MLA_Q_RANK + MLA_KV_RANK
    kr_w = jnp.pad(w_in[:, mla_cols:mla_cols + MLA_ROPE], ((0, 0), (MLA_NOPE, LANES - MLA_NOPE - MLA_ROPE)))
    w_mla = jnp.concatenate([w_in[:, :mla_cols], kr_w], axis=1).astype(BF16)
    w_rw = w_in[:, mla_cols + MLA_ROPE:].astype(BF16)
    per_q = MLA_NOPE + MLA_ROPE
    w_uq = f("mla_w_uq").reshape(MLA_Q_RANK, MLA_HEADS, per_q)
    w_uq = jnp.pad(w_uq, ((0, 0), (0, 0), (0, HEAD_PAD - per_q))).reshape(MLA_Q_RANK, MLA_HEADS * HEAD_PAD)
    w_ukv = f("mla_w_ukv").reshape(MLA_KV_RANK, MLA_HEADS, MLA_NOPE + MLA_V)
    w_uk = jnp.pad(w_ukv[:, :, :MLA_NOPE], ((0, 0), (0, 0), (0, HEAD_PAD - MLA_NOPE)))
    vv = w_ukv[:, :, MLA_NOPE:]
    v_even = jnp.pad(vv, ((0, 0), (0, 0), (0, HEAD_PAD - MLA_V)))
    v_odd = jnp.pad(vv, ((0, 0), (0, 0), (HEAD_PAD - MLA_V, 0)))
    odd = (jnp.arange(MLA_HEADS) % 2 == 1)[None, :, None]
    w_uv = jnp.where(odd, v_odd, v_even)
    gi = jnp.arange(LANES)
    grp = jnp.where(gi < MLA_NOPE, 0, jnp.where(gi < per_q, 1, 2))
    gmat_q = jnp.where((grp[:, None] == grp[None, :]) & (grp[:, None] < 2),
                       jnp.where(grp[:, None] == 0, 1.0 / MLA_NOPE, 1.0 / MLA_ROPE), 0.0)
    hid = jnp.arange(256) // RWKV_HEAD
    w_out = f("even_w_out").astype(BF16)
    zpad = jnp.zeros((RWKV_HEAD, RWKV_DIM), F32)
    return {
        "w_in": [w_mla, w_rw],
        "g_qlat": f("mla_g_qlat").reshape(1, -1), "g_kvlat": f("mla_g_kvlat").reshape(1, -1),
        "w_uq": w_uq.astype(BF16),
        "g_qhead": jnp.concatenate([f("mla_g_qn"), f("mla_g_qr"), jnp.zeros((HEAD_PAD - per_q,), F32)]).reshape(1, -1),
        "g_kr": _lane_row(f("mla_g_kr"), MLA_NOPE),
        "gmat_q": gmat_q.astype(BF16),
        "w_uk": w_uk.reshape(MLA_KV_RANK, -1).astype(BF16), "w_uv": w_uv.reshape(MLA_KV_RANK, -1).astype(BF16),
        "g_kn": _lane_row(f("mla_g_kn"), 0),
        "mu": f("rwkv_mu").reshape(1, -1), "w0": f("rwkv_w0").reshape(1, -1), "a0": f("rwkv_a0").reshape(1, -1),
        "w2": jnp.concatenate([f("rwkv_w2"), zpad], axis=0).astype(BF16),
        "a2": jnp.concatenate([zpad, f("rwkv_a2")], axis=0).astype(BF16),
        "g2": f("rwkv_g2").astype(BF16),
        "k_k": f("rwkv_k_k").reshape(1, -1), "k_a": f("rwkv_k_a").reshape(1, -1),
        "r_k": f("rwkv_r_k").reshape(1, -1),
        "lnx_g": f("rwkv_lnx_g").reshape(1, -1), "lnx_b": f("rwkv_lnx_b").reshape(1, -1),
        "gsum": (hid[:, None] == hid[None, :]).astype(BF16),
        "w_out1": w_out[:MLA_HEADS * MLA_V], "w_out2": w_out[MLA_HEADS * MLA_V:],
    }


def _prep_odd(W, i):
    f = lambda name: W[name][i]
    w_in = f("odd_w_in")
    c1 = POOL_DIM + GDN_QKV
    c2 = c1 + GDN_DIM
    w_out = f("odd_w_out").astype(BF16)
    return {
        "w_in": [w_in[:, :POOL_DIM].astype(BF16), w_in[:, POOL_DIM:c1].astype(BF16), w_in[:, c1:c2].astype(BF16),
                 _pad_cols(w_in[:, c2:], LANES).astype(BF16)],
        "pool_w": f("pool_w").astype(BF16), "pool_scale": f("pool_scale").reshape(1, -1),
        "conv_w": f("gdn_conv_w"),
        "a_log": _lane_row(f("gdn_a_log"), GDN_HEADS), "dt_bias": _lane_row(f("gdn_dt_bias"), GDN_HEADS),
        "o_g": f("gdn_o_g").reshape(1, -1),
        "sel": (jnp.arange(LANES)[None, :] == (jnp.arange(8)[:, None] + GDN_HEADS)).astype(BF16),
        "w_out1": w_out[:POOL_DIM], "w_out2": w_out[POOL_DIM:],
    }


def _rope_tabs(pos0, t):
    inv = 1.0 / (ROPE_THETA ** (jnp.arange(0, MLA_ROPE, 2, dtype=F32) / MLA_ROPE))
    ang = (pos0 + jnp.arange(t, dtype=jnp.int32)).astype(F32)[:, None] * inv[None, :]
    cos, sin = jnp.cos(ang), jnp.sin(ang)
    z16 = jnp.zeros_like(cos)
    one = jnp.ones((t, MLA_NOPE), F32)
    z64 = jnp.zeros((t, MLA_NOPE), F32)
    z32 = jnp.zeros((t, LANES - MLA_NOPE - MLA_ROPE), F32)
    return (jnp.concatenate([one, cos, cos, z32], axis=1),
            jnp.concatenate([z64, -sin, z16, z32], axis=1),
            jnp.concatenate([z64, z16, sin, z32], axis=1))


def _to_blockdiag(s):
    b = s.shape[0]
    s = s.reshape(b, RWKV_HEADS // 2, 2, RWKV_HEAD, RWKV_HEAD)
    z = jnp.zeros_like(s[:, :, 0])
    top = jnp.concatenate([s[:, :, 0], z], axis=-1)
    bot = jnp.concatenate([z, s[:, :, 1]], axis=-1)
    return jnp.concatenate([top, bot], axis=-2)


def _from_blockdiag(sb):
    b = sb.shape[0]
    s0 = sb[:, :, :RWKV_HEAD, :RWKV_HEAD]
    s1 = sb[:, :, RWKV_HEAD:, RWKV_HEAD:]
    return jnp.stack([s0, s1], axis=2).reshape(b, RWKV_HEADS, RWKV_HEAD, RWKV_HEAD)


def _trunk(x, mod, pos0, caches, W, prep):
    b, t, _ = x.shape
    ckv_c, kpe_c, shift_c, wkv_c, pool_c, conv_c, gdn_c, ffn_c = caches
    tabs = _rope_tabs(pos0, t)
    o_ckv, o_kpe, o_shift, o_wkv, o_pool, o_conv, o_gdn, o_ffn = [], [], [], [], [], [], [], []
    for layer in range(DEPTH):
        i = layer // 2
        wl = prep[layer]
        mod_l = mod[layer]
        if layer % 2 == 0:
            p_mla, rw = _inproj(x, mod_l, W["norm_mix_g"][layer].reshape(1, -1), wl["w_in"])
            q, ckv_new, kpe_new, kpe_blk = _mla_prep(p_mla, tabs, wl)
            if ckv_c is None:
                ckv_all, kblk_all, past = ckv_new, kpe_blk, 0
            else:
                past = ckv_c.shape[2]
                ckv_all = jnp.concatenate([ckv_c[i], ckv_new], axis=1)
                past_blk = jnp.pad(kpe_c[i], ((0, 0), (0, 0), (MLA_NOPE, LANES - MLA_NOPE - MLA_ROPE)))
                kblk_all = jnp.concatenate([past_blk, kpe_blk], axis=1)
            kk, vv = _mla_kv(ckv_all, kblk_all, wl)
            m1 = _attention(q, kk, vv, past)
            m2, sblk = _rwkv(rw, shift_c[i][:, None, :], _to_blockdiag(wkv_c[i]), wl)
            o_ckv.append(ckv_new)
            o_kpe.append(kpe_new)
            o_shift.append(rw[:, -1])
            o_wkv.append(_from_blockdiag(sblk))
        else:
            u, qkv, z, ba = _inproj(x, mod_l, W["norm_mix_g"][layer].reshape(1, -1), wl["w_in"])
            hist16 = jnp.pad(pool_c[i], ((0, 0), (1, 0), (0, 0)))
            m1 = _pool(u, hist16, pos0, wl)
            m2, gs = _gdn(qkv, z, ba, conv_c[i], gdn_c[i], wl)
            o_pool.append(u[:, t - POOL_HIST:])
            o_conv.append(qkv[:, t - (GDN_CONV - 1):])
            o_gdn.append(gs)
        x = _outproj(x, m1, m2, mod_l, wl["w_out1"], wl["w_out2"])
        x, fc = _ffn(x, mod_l, W["norm_ffn_g"][layer].reshape(1, -1), ffn_c[layer],
                     W["ffn_w_gate"][layer].astype(BF16), W["ffn_w_up"][layer].astype(BF16),
                     W["ffn_conv_w"][layer], W["ffn_w_down"][layer].astype(BF16))
        o_ffn.append(fc)
    st = jnp.stack
    return x, (st(o_ckv), st(o_kpe), st(o_shift), st(o_wkv), st(o_pool), st(o_conv), st(o_gdn), st(o_ffn))


def _zero_caches(b, dt):
    n_even, n_odd = (DEPTH + 1) // 2, DEPTH // 2
    return (None, None,
            jnp.zeros((n_even, b, RWKV_COLS), dt),
            jnp.zeros((n_even, b, RWKV_HEADS, RWKV_HEAD, RWKV_HEAD), dt),
            jnp.zeros((n_odd, b, POOL_HIST, POOL_DIM), dt),
            jnp.zeros((n_odd, b, GDN_CONV - 1, GDN_QKV), dt),
            jnp.zeros((n_odd, b, GDN_HEADS, GDN_DK, GDN_DV), dt),
            jnp.zeros((DEPTH, b, FFN_CONV - 1, D_FF), dt))


def _run(x_prompt, x_sample, caches_s, c_prompt, c_sample, W):
    bp = x_prompt.shape[0]
    mod = _ada(jnp.concatenate([c_prompt, c_sample], axis=0), W["ada_w"], W["ada_b"])
    mod = mod.reshape(DEPTH, -1, 6, D_MODEL)
    prep = [(_prep_even if l % 2 == 0 else _prep_odd)(W, l // 2) for l in range(DEPTH)]
    y_p, st_p = _trunk(x_prompt, mod[:, :bp], 0, _zero_caches(bp, x_prompt.dtype), W, prep)
    past_len = caches_s[0].shape[2]
    y_s, st_s = _trunk(x_sample, mod[:, bp:], past_len, caches_s, W, prep)
    return (y_p, y_s) + tuple(st_p) + tuple(st_s)


def kernel(x_prompt, x_sample, cache_mla_ckv, cache_mla_kpe, state_rwkv_shift, state_rwkv_wkv, state_pool,
           state_gdn_conv, state_gdn, state_ffn_conv, c_prompt, c_sample, ada_w, ada_b, norm_mix_g, norm_ffn_g,
           even_w_in, mla_g_qlat, mla_g_kvlat, mla_w_uq, mla_w_ukv, mla_g_qn, mla_g_qr, mla_g_kn, mla_g_kr,
           rwkv_mu, rwkv_w0, rwkv_w2, rwkv_a0, rwkv_a2, rwkv_g2, rwkv_k_k, rwkv_k_a, rwkv_r_k, rwkv_lnx_g,
           rwkv_lnx_b, even_w_out, odd_w_in, pool_w, pool_scale, gdn_conv_w, gdn_a_log, gdn_dt_bias, gdn_o_g,
           odd_w_out, ffn_w_gate, ffn_w_up, ffn_conv_w, ffn_w_down):
    W = {
        "ada_w": ada_w, "ada_b": ada_b, "norm_mix_g": norm_mix_g, "norm_ffn_g": norm_ffn_g,
        "even_w_in": even_w_in, "mla_g_qlat": mla_g_qlat, "mla_g_kvlat": mla_g_kvlat,
        "mla_w_uq": mla_w_uq, "mla_w_ukv": mla_w_ukv, "mla_g_qn": mla_g_qn, "mla_g_qr": mla_g_qr,
        "mla_g_kn": mla_g_kn, "mla_g_kr": mla_g_kr,
        "rwkv_mu": rwkv_mu, "rwkv_w0": rwkv_w0, "rwkv_w2": rwkv_w2, "rwkv_a0": rwkv_a0, "rwkv_a2": rwkv_a2,
        "rwkv_g2": rwkv_g2, "rwkv_k_k": rwkv_k_k, "rwkv_k_a": rwkv_k_a, "rwkv_r_k": rwkv_r_k,
        "rwkv_lnx_g": rwkv_lnx_g, "rwkv_lnx_b": rwkv_lnx_b, "even_w_out": even_w_out,
        "odd_w_in": odd_w_in, "pool_w": pool_w, "pool_scale": pool_scale, "gdn_conv_w": gdn_conv_w,
        "gdn_a_log": gdn_a_log, "gdn_dt_bias": gdn_dt_bias, "gdn_o_g": gdn_o_g, "odd_w_out": odd_w_out,
        "ffn_w_gate": ffn_w_gate, "ffn_w_up": ffn_w_up, "ffn_conv_w": ffn_conv_w, "ffn_w_down": ffn_w_down,
    }
    caches_s = (cache_mla_ckv, cache_mla_kpe, state_rwkv_shift, state_rwkv_wkv, state_pool, state_gdn_conv,
                state_gdn, state_ffn_conv)
    return _run(x_prompt, x_sample, caches_s, c_prompt, c_sample, W)
```

```python
import functools
import math

import jax
import jax.numpy as jnp
from jax import lax
from jax.experimental import pallas as pl
from jax.experimental.pallas import tpu as pltpu

F32 = jnp.float32
BF16 = jnp.bfloat16

D_MODEL = 1024
DEPTH = 4
CHUNK = 64
EPS = 1e-6
L2_EPS = 1e-6
NEG_INF = -1e30

MLA_HEADS = 8
MLA_NOPE = 64
MLA_ROPE = 32
MLA_V = 64
MLA_Q_RANK = 256
MLA_KV_RANK = 128
MLA_SCALE = (MLA_NOPE + MLA_ROPE) ** -0.5
ROPE_THETA = 10000.0

RWKV_HEADS = 8
RWKV_HEAD = 64
RWKV_DIM = 512
RWKV_COLS = 1792
RWKV_LN_EPS = 64e-5

POOL_DIM = 512
POOL_WINDOWS = (2, 4, 8, 16)
POOL_HIST = 15

GDN_HEADS = 4
GDN_DK = 128
GDN_DV = 128
GDN_CONV = 4
GDN_QKV = 1536
GDN_DIM = 512

D_FF = 2816
FFN_CONV = 3

LANES = 128
HEAD_PAD = 128
VMEM_LIMIT = 56 * 1024 * 1024
ATTN_HEAD_GROUP = 8


def _dot(a, b):
    return jnp.dot(a, b, preferred_element_type=F32)


def _dot_nt(a, b):
    return lax.dot_general(a, b, (((1,), (1,)), ((), ())), preferred_element_type=F32)


def _dot_tn(a, b):
    return lax.dot_general(a, b, (((0,), (0,)), ((), ())), preferred_element_type=F32)


def _split(x, n):
    parts = []
    r = x
    for i in range(n):
        p = r.astype(BF16)
        parts.append(p)
        if i + 1 < n:
            r = r - p.astype(F32)
    return parts


def _sum_left(mat01, x, n):
    out = None
    for p in _split(x, n):
        d = _dot(mat01, p)
        out = d if out is None else out + d
    return out


def _sum_right(x, mat01, n):
    out = None
    for p in _split(x, n):
        d = _dot(p, mat01)
        out = d if out is None else out + d
    return out


def _sigmoid(x):
    return 1.0 / (1.0 + jnp.exp(-x))


def _silu(x):
    return x * _sigmoid(x)


def _softplus(x):
    return jnp.maximum(x, 0.0) + jnp.log(1.0 + jnp.exp(-jnp.abs(x)))


def _rms(x, gain, eps=EPS):
    return x * lax.rsqrt(jnp.mean(x * x, axis=-1, keepdims=True) + eps) * gain


def _modulate(x, gain, shift, scale):
    return _rms(x, gain) * (1.0 + scale) + shift


def _solve_unit_lower(ns, rhss, steps):
    size = ns[0].shape[0]
    ri = lax.broadcasted_iota(jnp.int32, (size, size), 0)
    ci = lax.broadcasted_iota(jnp.int32, (size, size), 1)
    eye = jnp.where(ri == ci, 1.0, 0.0)
    xs = [eye + n for n in ns]
    pbs = [n.astype(BF16) for n in ns]
    pbs = [_dot(p, p).astype(BF16) for p in pbs]
    for _ in range(steps - 1):
        xs = [x + _dot(x.astype(BF16), p) for x, p in zip(xs, pbs)]
        pbs = [_dot(p, p).astype(BF16) for p in pbs]
    xbs = [(x + _dot(x.astype(BF16), p)).astype(BF16) for x, p in zip(xs, pbs)]
    us = [_dot(xb, r.astype(BF16)) for xb, r in zip(xbs, rhss)]
    nbs = [n.astype(BF16) for n in ns]
    usp = [_split(u, 2) for u in us]
    nus = [_dot(nb, uh) for nb, (uh, _) in zip(nbs, usp)]
    nus = [nu + _dot(nb, ul) for nu, nb, (_, ul) in zip(nus, nbs, usp)]
    return [u + _dot(xb, (r - u + nu).astype(BF16)) for u, xb, r, nu in zip(us, xbs, rhss, nus)]


def _const_spec(shape):
    nd = len(shape)
    return pl.BlockSpec(shape, lambda *_: (0,) * nd, pipeline_mode=pl.Buffered(1))


def _params(n_axes):
    return pltpu.CompilerParams(dimension_semantics=("arbitrary",) * n_axes, vmem_limit_bytes=VMEM_LIMIT)


def _time_tile(t, pref):
    return pref if t % pref == 0 else t


def _ada_kernel(c_ref, w_ref, b_ref, o_ref):
    s = _silu(c_ref[...]).astype(BF16)
    o_ref[0] = _dot(s, w_ref[0].astype(BF16)) + b_ref[0]


def _ada(c_all, ada_w, ada_b):
    n = c_all.shape[0]
    tn = 1536
    return pl.pallas_call(
        _ada_kernel,
        out_shape=jax.ShapeDtypeStruct((DEPTH, n, 6 * D_MODEL), F32),
        grid=(DEPTH, 6 * D_MODEL // tn),
        in_specs=[
            pl.BlockSpec((n, D_MODEL), lambda l, j: (0, 0)),
            pl.BlockSpec((1, D_MODEL, tn), lambda l, j: (l, 0, j)),
            pl.BlockSpec((1, 1, tn), lambda l, j: (l, 0, j)),
        ],
        out_specs=pl.BlockSpec((1, n, tn), lambda l, j: (l, 0, j)),
        compiler_params=_params(2),
        name="ada_mod",
    )(c_all, ada_w, ada_b.reshape(DEPTH, 1, 6 * D_MODEL))


def _inproj_kernel(x_ref, mod_ref, g_ref, *refs, nseg):
    m = mod_ref[0]
    h = _modulate(x_ref[0], g_ref[...], m[0:1], m[1:2]).astype(BF16)
    for w_ref, o_ref in zip(refs[:nseg], refs[nseg:]):
        o_ref[0] = _dot(h, w_ref[...])


def _inproj(x, mod_l, gain, w_segs):
    b, t, d = x.shape
    tm = _time_tile(t, 512)
    nseg = len(w_segs)
    return pl.pallas_call(
        functools.partial(_inproj_kernel, nseg=nseg),
        out_shape=[jax.ShapeDtypeStruct((b, t, w.shape[1]), F32) for w in w_segs],
        grid=(b, t // tm),
        in_specs=[
            pl.BlockSpec((1, tm, d), lambda i, j: (i, j, 0)),
            pl.BlockSpec((1, 6, d), lambda i, j: (i, 0, 0)),
            _const_spec((1, d)),
        ] + [_const_spec(w.shape) for w in w_segs],
        out_specs=[pl.BlockSpec((1, tm, w.shape[1]), lambda i, j: (i, j, 0)) for w in w_segs],
        compiler_params=_params(2),
        name="in_proj",
    )(x, mod_l, gain, *w_segs)


def _rope(blk, c, sm, sp):
    return blk * c + pltpu.roll(blk, LANES - 16, 1) * sm + pltpu.roll(blk, 16, 1) * sp


def _mla_prep_kernel(p_ref, c_ref, sm_ref, sp_ref, gq_ref, gkv_ref, wuq_ref, ghead_ref, gkr_ref, gmat_ref,
                     q_ref, ckv_ref, kpe_ref, kblk_ref):
    p = p_ref[0]
    c, sm, sp = c_ref[...], sm_ref[...], sp_ref[...]
    qn = _rms(p[:, :MLA_Q_RANK], gq_ref[...]).astype(BF16)
    q = _dot(qn, wuq_ref[...])
    gmat = gmat_ref[...]
    for h in range(MLA_HEADS):
        hb = q[:, h * HEAD_PAD:(h + 1) * HEAD_PAD]
        ms = _sum_right(hb * hb, gmat, 2)
        hb = hb * lax.rsqrt(ms + EPS) * ghead_ref[...]
        q_ref[0, h] = _rope(hb, c, sm, sp).astype(BF16)
    ckv_ref[0] = _rms(p[:, MLA_Q_RANK:MLA_Q_RANK + MLA_KV_RANK], gkv_ref[...])
    kr = p[:, MLA_Q_RANK + MLA_KV_RANK:]
    ms = jnp.sum(kr * kr, axis=-1, keepdims=True) * (1.0 / MLA_ROPE)
    kb = _rope(kr * lax.rsqrt(ms + EPS) * gkr_ref[...], c, sm, sp)
    kblk_ref[0] = kb
    kpe_ref[0] = kb[:, MLA_NOPE:MLA_NOPE + MLA_ROPE]


def _mla_prep(p_mla, tabs, wl):
    b, t, _ = p_mla.shape
    tm = _time_tile(t, 512)
    tab_spec = pl.BlockSpec((tm, LANES), lambda i, j: (j, 0))
    return pl.pallas_call(
        _mla_prep_kernel,
        out_shape=[
            jax.ShapeDtypeStruct((b, MLA_HEADS, t, HEAD_PAD), BF16),
            jax.ShapeDtypeStruct((b, t, MLA_KV_RANK), F32),
            jax.ShapeDtypeStruct((b, t, MLA_ROPE), F32),
            jax.ShapeDtypeStruct((b, t, LANES), F32),
        ],
        grid=(b, t // tm),
        in_specs=[
            pl.BlockSpec((1, tm, 512), lambda i, j: (i, j, 0)),
            tab_spec, tab_spec, tab_spec,
            _const_spec((1, MLA_Q_RANK)), _const_spec((1, MLA_KV_RANK)),
            _const_spec((MLA_Q_RANK, MLA_HEADS * HEAD_PAD)),
            _const_spec((1, LANES)), _const_spec((1, LANES)), _const_spec((LANES, LANES)),
        ],
        out_specs=[
            pl.BlockSpec((1, MLA_HEADS, tm, HEAD_PAD), lambda i, j: (i, 0, j, 0)),
            pl.BlockSpec((1, tm, MLA_KV_RANK), lambda i, j: (i, j, 0)),
            pl.BlockSpec((1, tm, MLA_ROPE), lambda i, j: (i, j, 0)),
            pl.BlockSpec((1, tm, LANES), lambda i, j: (i, j, 0)),
        ],
        compiler_params=_params(2),
        name="mla_prep",
    )(p_mla, tabs[0], tabs[1], tabs[2], wl["g_qlat"], wl["g_kvlat"], wl["w_uq"], wl["g_qhead"], wl["g_kr"],
      wl["gmat_q"])


def _mla_kv_kernel(ckv_ref, kpe_ref, wuk_ref, wuv_ref, gkn_ref, k_ref, v_ref):
    cb = ckv_ref[0].astype(BF16)
    kf = _dot(cb, wuk_ref[...])
    vf = _dot(cb, wuv_ref[...])
    kpe = kpe_ref[0]
    for h in range(MLA_HEADS):
        kb = kf[:, h * HEAD_PAD:(h + 1) * HEAD_PAD]
        ms = jnp.sum(kb * kb, axis=-1, keepdims=True) * (1.0 / MLA_NOPE)
        k_ref[0, h] = (kb * lax.rsqrt(ms + EPS) * gkn_ref[...] + kpe).astype(BF16)
        v_ref[0, h] = vf[:, h * HEAD_PAD:(h + 1) * HEAD_PAD].astype(BF16)


def _mla_kv(ckv_all, kpe_blk, wl):
    b, l, _ = ckv_all.shape
    tl = _time_tile(l, 512)
    return pl.pallas_call(
        _mla_kv_kernel,
        out_shape=[jax.ShapeDtypeStruct((b, MLA_HEADS, l, HEAD_PAD), BF16)] * 2,
        grid=(b, l // tl),
        in_specs=[
            pl.BlockSpec((1, tl, MLA_KV_RANK), lambda i, j: (i, j, 0)),
            pl.BlockSpec((1, tl, LANES), lambda i, j: (i, j, 0)),
            _const_spec((MLA_KV_RANK, MLA_HEADS * HEAD_PAD)),
            _const_spec((MLA_KV_RANK, MLA_HEADS * HEAD_PAD)),
            _const_spec((1, LANES)),
        ],
        out_specs=[pl.BlockSpec((1, MLA_HEADS, tl, HEAD_PAD), lambda i, j: (i, 0, j, 0))] * 2,
        compiler_params=_params(2),
        name="mla_kv",
    )(ckv_all, kpe_blk, wl["w_uk"], wl["w_uv"], wl["g_kn"])


def _attn_kernel(q_ref, k_ref, v_ref, o_ref, *, tq, tk, past_len):
    d0 = pl.multiple_of(past_len + pl.program_id(1) * tq, tq)
    nprefix = d0 // tk
    qc = lax.broadcasted_iota(jnp.int32, (tq, tq), 0) // CHUNK
    kc = lax.broadcasted_iota(jnp.int32, (tq, tq), 1) // CHUNK
    mask = kc <= qc
    for g0 in range(0, MLA_HEADS, ATTN_HEAD_GROUP):
        _attn_heads(q_ref, k_ref, v_ref, o_ref, range(g0, g0 + ATTN_HEAD_GROUP), d0, nprefix, mask, tq, tk)


def _attn_heads(q_ref, k_ref, v_ref, o_ref, heads, d0, nprefix, mask, tq, tk):
    ss = [_dot_nt(q_ref[0, h], k_ref[0, h, pl.ds(d0, tq), :]) for h in heads]
    ss = [jnp.where(mask, s * MLA_SCALE, NEG_INF) for s in ss]
    ms = [jnp.max(s, axis=-1, keepdims=True) for s in ss]
    ps = [jnp.exp(s - m) for s, m in zip(ss, ms)]
    ls = [jnp.sum(p, axis=-1, keepdims=True) for p in ps]
    accs = [_dot(p.astype(BF16), v_ref[0, h, pl.ds(d0, tq), :]) for p, h in zip(ps, heads)]

    def body(j, carry):
        ms, ls, accs = carry
        r0 = pl.multiple_of(j * tk, tk)
        ss = [_dot_nt(q_ref[0, h], k_ref[0, h, pl.ds(r0, tk), :]) * MLA_SCALE for h in heads]
        mns = [jnp.maximum(m, jnp.max(s, axis=-1, keepdims=True)) for m, s in zip(ms, ss)]
        alphas = [jnp.exp(m - mn) for m, mn in zip(ms, mns)]
        ps = [jnp.exp(s - mn) for s, mn in zip(ss, mns)]
        ls = [a * l + jnp.sum(p, axis=-1, keepdims=True) for a, l, p in zip(alphas, ls, ps)]
        pvs = [_dot(p.astype(BF16), v_ref[0, h, pl.ds(r0, tk), :]) for p, h in zip(ps, heads)]
        accs = [a * acc + pv for a, acc, pv in zip(alphas, accs, pvs)]
        return tuple(mns), tuple(ls), tuple(accs)

    ms, ls, accs = lax.fori_loop(0, nprefix, body, (tuple(ms), tuple(ls), tuple(accs)))
    outs = [acc / l for acc, l in zip(accs, ls)]
    for i in range(len(outs) // 2):
        j = heads[0] // 2 + i
        o_ref[0, :, j * LANES:(j + 1) * LANES] = outs[2 * i] + outs[2 * i + 1]


def _attention(q, k, v, past_len):
    b, hh, t, _ = q.shape
    l = k.shape[2]
    tq = _time_tile(t, 256)
    tk = 256
    assert past_len % tk == 0 and (tq % tk == 0 or past_len + tq == l) and l == past_len + t
    return pl.pallas_call(
        functools.partial(_attn_kernel, tq=tq, tk=tk, past_len=past_len),
        out_shape=jax.ShapeDtypeStruct((b, t, MLA_HEADS * MLA_V), F32),
        grid=(b, t // tq),
        in_specs=[
            pl.BlockSpec((1, hh, tq, HEAD_PAD), lambda i, j: (i, 0, j, 0)),
            pl.BlockSpec((1, hh, l, HEAD_PAD), lambda i, j: (i, 0, 0, 0)),
            pl.BlockSpec((1, hh, l, HEAD_PAD), lambda i, j: (i, 0, 0, 0)),
        ],
        out_specs=pl.BlockSpec((1, tq, MLA_HEADS * MLA_V), lambda i, j: (i, j, 0)),
        compiler_params=_params(2),
        name="mla_attn",
    )(q, k, v)


def _rwkv_kernel(rw_ref, sh_ref, s0_ref, mu_ref, w0_ref, w2_ref, a0_ref, a2_ref, g2_ref, kkw_ref, ka_ref,
                 rk_ref, lng_ref, lnb_ref, gsum_ref, y_ref, s_ref, xs_scr, st_scr, *, tt, cc):
    t = pl.program_id(1)

    @pl.when(t == 0)
    def _():
        xs_scr[7:8, :] = sh_ref[0]
        st_scr[...] = s0_ref[0]

    rw = rw_ref[0]
    xs_scr[8:8 + tt, :] = rw
    prev = xs_scr[7:7 + tt, :]
    xm = rw + (prev - rw) * mu_ref[...]
    xs_scr[7:8, :] = rw[tt - 1:tt, :]

    r = xm[:, 0:RWKV_DIM]
    k = xm[:, RWKV_DIM:2 * RWKV_DIM]
    v = xm[:, 2 * RWKV_DIM:3 * RWKV_DIM]
    lw = xm[:, 3 * RWKV_DIM:3 * RWKV_DIM + LANES]
    dg = xm[:, 3 * RWKV_DIM + LANES:]
    zw = w0_ref[...] + _dot(jnp.tanh(lw).astype(BF16), w2_ref[...])
    ld = -jnp.exp(-_softplus(-zw) - 0.5)
    a = _sigmoid(a0_ref[...] + _dot(lw.astype(BF16), a2_ref[...]))
    g = _dot(_sigmoid(dg).astype(BF16), g2_ref[...])

    gmat = gsum_ref[...]

    def gsum(x):
        return jnp.concatenate([_sum_right(x[:, :256], gmat, 2), _sum_right(x[:, 256:], gmat, 2)], axis=-1)

    kkr = k * kkw_ref[...]
    kk = kkr * lax.rsqrt(gsum(kkr * kkr) + L2_EPS)
    k2 = k * (1.0 + (a - 1.0) * ka_ref[...])
    kka = kk * a
    bonus = gsum(r * k2 * rk_ref[...]) * v

    c2 = 2 * cc
    nch = tt // cc
    npair = RWKV_HEADS // 2
    ri = lax.broadcasted_iota(jnp.int32, (tt, tt), 0)
    ci = lax.broadcasted_iota(jnp.int32, (tt, tt), 1)
    tri = jnp.where(ci <= ri, jnp.where(ci // cc == ri // cc, 1.0, 0.0), 0.0).astype(BF16)
    rl = lax.broadcasted_iota(jnp.int32, (c2, 2 * c2), 0) % cc
    cl = lax.broadcasted_iota(jnp.int32, (c2, 2 * c2), 1) % cc
    strict = cl < rl
    incl = cl <= rl
    low = lax.broadcasted_iota(jnp.int32, (1, LANES), 1) < RWKV_HEAD

    cs = _sum_left(tri, ld, 3)
    tots = [cs[(c + 1) * cc - 1:(c + 1) * cc, :] for c in range(nch)]
    tot = jnp.concatenate([jnp.broadcast_to(tc, (cc, RWKV_DIM)) for tc in tots], axis=0) if nch > 1 else tots[0]
    pinv = jnp.exp(-cs)
    pend = jnp.exp(tot - cs)
    at = -(kk * jnp.exp(cs - ld))
    rt = r * jnp.exp(cs)
    bt = kka * pinv
    kt = k2 * pinv
    bh = kka * pend
    kh = k2 * pend

    inst = [(c, j) for c in range(nch) for j in range(npair)]

    def blk(x, c, j):
        xp = x[c * cc:(c + 1) * cc, j * LANES:(j + 1) * LANES]
        return jnp.concatenate([jnp.where(low, xp, 0.0), jnp.where(low, 0.0, xp)], axis=0)

    a_b = [blk(at, c, j) for c, j in inst]
    r_b = [blk(rt, c, j) for c, j in inst]
    v_b = [blk(v, c, j) for c, j in inst]
    bk_b = [jnp.concatenate([blk(bh, c, j), blk(kh, c, j)], axis=0).astype(BF16) for c, j in inst]
    lhs = [jnp.concatenate([a, rr], axis=0).astype(BF16) for a, rr in zip(a_b, r_b)]
    rhs = [jnp.concatenate([blk(bt, c, j), blk(kt, c, j)], axis=0).astype(BF16) for c, j in inst]
    mm = [_dot_nt(l, rr) for l, rr in zip(lhs, rhs)]
    mt = [jnp.where(strict, m[:c2], 0.0) for m in mm]
    mb = [jnp.where(incl, m[c2:], 0.0).astype(BF16) for m in mm]
    akv = [_dot(m.astype(BF16), jnp.concatenate([jnp.zeros_like(vb), vb], axis=0).astype(BF16))
           for m, vb in zip(mt, v_b)]
    zs = _solve_unit_lower([m[:, :c2] for m in mt], [jnp.concatenate([a, w], axis=1) for a, w in zip(a_b, akv)],
                           int(math.log2(cc)) - 1)
    ahat = [z[:, :LANES].astype(BF16) for z in zs]
    wv = [jnp.concatenate([z[:, LANES:], vb], axis=0).astype(BF16) for z, vb in zip(zs, v_b)]
    rhat = [(rr + _dot(m[:, :c2], ah)).astype(BF16) for rr, m, ah in zip(r_b, mb, ahat)]
    yc = [_dot(m, w) for m, w in zip(mb, wv)]
    phi = [_dot_tn(ah, bk[:c2]).astype(BF16) for ah, bk in zip(ahat, bk_b)]
    gam = [_dot_tn(w, bk) for w, bk in zip(wv, bk_b)]

    sb = [st_scr[j] for j in range(npair)]
    ys = []
    for c in range(nch):
        ptot = jnp.exp(tots[c])
        sbb = [s.astype(BF16) for s in sb]
        yb = [_dot_nt(rhat[c * npair + j], sbb[j]) + yc[c * npair + j] for j in range(npair)]
        sb = [sb[j] * ptot[:, j * LANES:(j + 1) * LANES] + _dot(sbb[j], phi[c * npair + j]) + gam[c * npair + j]
              for j in range(npair)]
        ys.append(jnp.concatenate([yy[:cc] + yy[cc:] for yy in yb], axis=-1))
    for j in range(npair):
        st_scr[j] = sb[j]
    y = jnp.concatenate(ys, axis=0) if nch > 1 else ys[0]

    mu = gsum(y) * (1.0 / RWKV_HEAD)
    dy = y - mu
    var = gsum(dy * dy) * (1.0 / RWKV_HEAD)
    yn = dy * lax.rsqrt(var + RWKV_LN_EPS) * lng_ref[...] + lnb_ref[...]
    y_ref[0] = (yn + bonus) * g
    s_ref[0] = st_scr[...]


def _rwkv(rw, shift, sblk, wl):
    b, t, _ = rw.shape
    cc = min(t, CHUNK)
    tt = _time_tile(t, 256)
    row = lambda n: _const_spec((1, n))
    return pl.pallas_call(
        functools.partial(_rwkv_kernel, tt=tt, cc=cc),
        out_shape=[
            jax.ShapeDtypeStruct((b, t, RWKV_DIM), F32),
            jax.ShapeDtypeStruct((b, RWKV_HEADS // 2, LANES, LANES), F32),
        ],
        grid=(b, t // tt),
        in_specs=[
            pl.BlockSpec((1, tt, RWKV_COLS), lambda i, j: (i, j, 0)),
            pl.BlockSpec((1, 1, RWKV_COLS), lambda i, j: (i, 0, 0)),
            pl.BlockSpec((1, RWKV_HEADS // 2, LANES, LANES), lambda i, j: (i, 0, 0, 0)),
            row(RWKV_COLS), row(RWKV_DIM), _const_spec((LANES, RWKV_DIM)), row(RWKV_DIM),
            _const_spec((LANES, RWKV_DIM)), _const_spec((LANES, RWKV_DIM)),
            row(RWKV_DIM), row(RWKV_DIM), row(RWKV_DIM), row(RWKV_DIM), row(RWKV_DIM),
            _const_spec((256, 256)),
        ],
        out_specs=[
            pl.BlockSpec((1, tt, RWKV_DIM), lambda i, j: (i, j, 0)),
            pl.BlockSpec((1, RWKV_HEADS // 2, LANES, LANES), lambda i, j: (i, 0, 0, 0)),
        ],
        scratch_shapes=[
            pltpu.VMEM((tt + 8, RWKV_COLS), F32),
            pltpu.VMEM((RWKV_HEADS // 2, LANES, LANES), F32),
        ],
        compiler_params=_params(2),
        name="rwkv7",
    )(rw, shift, sblk, wl["mu"], wl["w0"], wl["w2"], wl["a0"], wl["a2"], wl["g2"], wl["k_k"], wl["k_a"],
      wl["r_k"], wl["lnx_g"], wl["lnx_b"], wl["gsum"])


def _pool_kernel(u_ref, hist_ref, pw_ref, ps_ref, o_ref, scr, *, tt, pos0):
    t = pl.program_id(1)

    @pl.when(t == 0)
    def _():
        scr[0:16, :] = hist_ref[0]

    u = u_ref[0]
    scr[16:16 + tt, :] = u
    pos = pos0 + t * tt + lax.broadcasted_iota(jnp.int32, (tt, 1), 0)
    for gi, w in enumerate(POOL_WINDOWS):
        ls = slice(gi * LANES, (gi + 1) * LANES)
        ug = u[:, ls]
        s = ug
        for d in range(1, w):
            s = s + scr[16 - d:16 - d + tt, ls]
        cnt = jnp.minimum(w, pos + 1).astype(F32)
        diff = s / cnt - ug
        o_ref[0, :, ls] = _dot(diff.astype(BF16), pw_ref[gi]) * ps_ref[:, ls]
    scr[0:16, :] = scr[tt:tt + 16, :]


def _pool(u, hist16, pos0, wl):
    b, t, _ = u.shape
    tt = _time_tile(t, 256)
    return pl.pallas_call(
        functools.partial(_pool_kernel, tt=tt, pos0=pos0),
        out_shape=jax.ShapeDtypeStruct((b, t, POOL_DIM), F32),
        grid=(b, t // tt),
        in_specs=[
            pl.BlockSpec((1, tt, POOL_DIM), lambda i, j: (i, j, 0)),
            pl.BlockSpec((1, 16, POOL_DIM), lambda i, j: (i, 0, 0)),
            _const_spec((len(POOL_WINDOWS), LANES, LANES)),
            _const_spec((1, POOL_DIM)),
        ],
        out_specs=pl.BlockSpec((1, tt, POOL_DIM), lambda i, j: (i, j, 0)),
        scratch_shapes=[pltpu.VMEM((tt + 16, POOL_DIM), F32)],
        compiler_params=_params(2),
        name="pool_mix",
    )(u, hist16, wl["pool_w"], wl["pool_scale"])


def _gdn_kernel(qkv_ref, z_ref, ba_ref, hist_ref, s0_ref, cw_ref, alog_ref, dtb_ref, og_ref, sel_ref,
                o_ref, s_ref, x_scr, st_scr, *, tt, cc):
    t = pl.program_id(1)

    @pl.when(t == 0)
    def _():
        x_scr[5:8, :] = hist_ref[0]
        st_scr[...] = s0_ref[0]

    x = qkv_ref[0]
    x_scr[8:8 + tt, :] = x
    cw = cw_ref[...]
    xc = cw[3:4] * x + cw[2:3] * x_scr[7:7 + tt, :] + cw[1:2] * x_scr[6:6 + tt, :] + cw[0:1] * x_scr[5:5 + tt, :]
    x_scr[5:8, :] = x[tt - 3:tt, :]
    xc = _silu(xc)
    ba = ba_ref[0]
    beta_all = _sigmoid(ba)
    g_all = -jnp.exp(alog_ref[...]) * _softplus(ba + dtb_ref[...])

    nch = tt // cc
    ri = lax.broadcasted_iota(jnp.int32, (cc, cc), 0)
    ci = lax.broadcasted_iota(jnp.int32, (cc, cc), 1)
    incl = ci <= ri
    strict = ci < ri
    rt_ = lax.broadcasted_iota(jnp.int32, (tt, tt), 0)
    ct_ = lax.broadcasted_iota(jnp.int32, (tt, tt), 1)
    tri = jnp.where(ct_ <= rt_, jnp.where(ct_ // cc == rt_ // cc, 1.0, 0.0), 0.0).astype(BF16)
    sel = sel_ref[...]
    zz = z_ref[0]

    gcs = _sum_left(tri, g_all, 3)
    grows = []
    for c in range(nch):
        parts = _split(gcs[c * cc:(c + 1) * cc], 3)
        grows.append(_dot_nt(sel, parts[0]) + _dot_nt(sel, parts[1]) + _dot_nt(sel, parts[2]))

    qn, kn = [], []
    for h in range(GDN_HEADS):
        qh = xc[:, h * GDN_DK:(h + 1) * GDN_DK]
        kh = xc[:, (GDN_HEADS + h) * GDN_DK:(GDN_HEADS + h + 1) * GDN_DK]
        qn.append(qh * lax.rsqrt(jnp.sum(qh * qh, axis=-1, keepdims=True) + L2_EPS) * (GDN_DK ** -0.5))
        kn.append(kh * lax.rsqrt(jnp.sum(kh * kh, axis=-1, keepdims=True) + L2_EPS))

    inst = [(c, h) for c in range(nch) for h in range(GDN_HEADS)]
    v0 = 2 * GDN_HEADS * GDN_DK

    qs = [qn[h][c * cc:(c + 1) * cc] for c, h in inst]
    ks = [kn[h][c * cc:(c + 1) * cc] for c, h in inst]
    vs = [xc[c * cc:(c + 1) * cc, v0 + h * GDN_DV:v0 + (h + 1) * GDN_DV] for c, h in inst]
    betas = [beta_all[c * cc:(c + 1) * cc, h:h + 1] for c, h in inst]
    gcols = [gcs[c * cc:(c + 1) * cc, GDN_HEADS + h:GDN_HEADS + h + 1] for c, h in inst]
    glasts = [g[cc - 1:cc, :] for g in gcols]
    egs = [jnp.exp(g) for g in gcols]
    decs = [jnp.where(incl, jnp.exp(jnp.where(incl, g - grows[c][h:h + 1, :], 0.0)), 0.0)
            for g, (c, h) in zip(gcols, inst)]
    grams = [_dot_nt(jnp.concatenate([k_, q_], axis=0).astype(BF16), k_.astype(BF16)) for k_, q_ in zip(ks, qs)]
    n_mats = [-(b_ * gm[:cc] * jnp.where(strict, d_, 0.0)) for b_, gm, d_ in zip(betas, grams, decs)]
    qks = [(gm[cc:] * d_).astype(BF16) for gm, d_ in zip(grams, decs)]
    zs = _solve_unit_lower(n_mats, [jnp.concatenate([b_ * e_ * k_, b_ * v_], axis=1)
                                    for b_, e_, k_, v_ in zip(betas, egs, ks, vs)], int(math.log2(cc)) - 1)
    zb = [z.astype(BF16) for z in zs]
    qkz = [_dot(qk, z) for qk, z in zip(qks, zb)]
    qhat = [(e_ * q_ - t_[:, :GDN_DK]).astype(BF16) for e_, q_, t_ in zip(egs, qs, qkz)]
    pg = [_dot_tn((k_ * jnp.exp(gl - g)).astype(BF16), z) for k_, gl, g, z in zip(ks, glasts, gcols, zb)]

    sts = [st_scr[h] for h in range(GDN_HEADS)]
    os_ = []
    for c in range(nch):
        i0 = c * GDN_HEADS
        stb = [s_.astype(BF16) for s_ in sts]
        oo = [_dot(qhat[i0 + h], stb[h]) + qkz[i0 + h][:, GDN_DK:] for h in range(GDN_HEADS)]
        sts = [jnp.exp(glasts[i0 + h]) * sts[h] - _dot(pg[i0 + h][:, :GDN_DK].astype(BF16), stb[h])
               + pg[i0 + h][:, GDN_DK:] for h in range(GDN_HEADS)]
        os_.append(jnp.concatenate(
            [_rms(oo[h], og_ref[...]) * _silu(zz[c * cc:(c + 1) * cc, h * GDN_DV:(h + 1) * GDN_DV])
             for h in range(GDN_HEADS)], axis=-1))
    for h in range(GDN_HEADS):
        st_scr[h] = sts[h]
    o_ref[0] = jnp.concatenate(os_, axis=0) if nch > 1 else os_[0]
    s_ref[0] = st_scr[...]


def _gdn(qkv, z, ba, conv_hist, state, wl):
    b, t, _ = qkv.shape
    cc = min(t, CHUNK)
    tt = _time_tile(t, 256)
    return pl.pallas_call(
        functools.partial(_gdn_kernel, tt=tt, cc=cc),
        out_shape=[
            jax.ShapeDtypeStruct((b, t, GDN_DIM), F32),
            jax.ShapeDtypeStruct((b, GDN_HEADS, GDN_DK, GDN_DV), F32),
        ],
        grid=(b, t // tt),
        in_specs=[
            pl.BlockSpec((1, tt, GDN_QKV), lambda i, j: (i, j, 0)),
            pl.BlockSpec((1, tt, GDN_DIM), lambda i, j: (i, j, 0)),
            pl.BlockSpec((1, tt, LANES), lambda i, j: (i, j, 0)),
            pl.BlockSpec((1, GDN_CONV - 1, GDN_QKV), lambda i, j: (i, 0, 0)),
            pl.BlockSpec((1, GDN_HEADS, GDN_DK, GDN_DV), lambda i, j: (i, 0, 0, 0)),
            _const_spec((GDN_CONV, GDN_QKV)),
            _const_spec((1, LANES)), _const_spec((1, LANES)), _const_spec((1, GDN_DV)),
            _const_spec((8, LANES)),
        ],
        out_specs=[
            pl.BlockSpec((1, tt, GDN_DIM), lambda i, j: (i, j, 0)),
            pl.BlockSpec((1, GDN_HEADS, GDN_DK, GDN_DV), lambda i, j: (i, 0, 0, 0)),
        ],
        scratch_shapes=[
            pltpu.VMEM((tt + 8, GDN_QKV), F32),
            pltpu.VMEM((GDN_HEADS, GDN_DK, GDN_DV), F32),
        ],
        compiler_params=_params(2),
        name="gdn",
    )(qkv, z, ba, conv_hist, state, wl["conv_w"], wl["a_log"], wl["dt_bias"], wl["o_g"], wl["sel"])


def _block_kernel(x_ref, m1_ref, m2_ref, mod_ref, g_ref, hist_ref, wo1_ref, wo2_ref, wg_ref, wu_ref, cw_ref, wd_ref,
                  o_ref, hn_ref, a_scr, *, tm, nfc):
    t = pl.program_id(1)

    @pl.when(t == 0)
    def _():
        a_scr[6:8, :] = hist_ref[0]

    m = mod_ref[0]
    mix = _dot(m1_ref[0].astype(BF16), wo1_ref[...]) + _dot(m2_ref[0].astype(BF16), wo2_ref[...])
    x = x_ref[0] + m[2:3] * mix
    h = _modulate(x, g_ref[...], m[3:4], m[4:5]).astype(BF16)
    fc = D_FF // nfc
    acc = None
    for c in range(nfc):
        cs = slice(c * fc, (c + 1) * fc)
        a = _dot(h, wg_ref[:, cs])
        a_scr[8:8 + tm, cs] = a
        ac = cw_ref[2:3, cs] * a + cw_ref[1:2, cs] * a_scr[7:7 + tm, cs] + cw_ref[0:1, cs] * a_scr[6:6 + tm, cs]
        a_scr[6:8, cs] = a[tm - 2:tm, :]
        act = (_silu(ac) * _dot(h, wu_ref[:, cs])).astype(BF16)
        d = _dot(act, wd_ref[cs, :])
        acc = d if acc is None else acc + d
    o_ref[0] = x + m[5:6] * acc
    hn_ref[0] = a_scr[6:8, :]


def _block(x, m1, m2, mod_l, gain, hist, wo1, wo2, wg, wu, cw, wd):
    b, t, d = x.shape
    tm = _time_tile(t, 512)
    half = m1.shape[2]
    return pl.pallas_call(
        functools.partial(_block_kernel, tm=tm, nfc=2),
        out_shape=[
            jax.ShapeDtypeStruct((b, t, d), F32),
            jax.ShapeDtypeStruct((b, FFN_CONV - 1, D_FF), F32),
        ],
        grid=(b, t // tm),
        in_specs=[
            pl.BlockSpec((1, tm, d), lambda i, j: (i, j, 0)),
            pl.BlockSpec((1, tm, half), lambda i, j: (i, j, 0)),
            pl.BlockSpec((1, tm, half), lambda i, j: (i, j, 0)),
            pl.BlockSpec((1, 6, d), lambda i, j: (i, 0, 0)),
            _const_spec((1, d)),
            pl.BlockSpec((1, FFN_CONV - 1, D_FF), lambda i, j: (i, 0, 0)),
            _const_spec((half, d)), _const_spec((half, d)),
            _const_spec((d, D_FF)), _const_spec((d, D_FF)), _const_spec((FFN_CONV, D_FF)), _const_spec((D_FF, d)),
        ],
        out_specs=[
            pl.BlockSpec((1, tm, d), lambda i, j: (i, j, 0)),
            pl.BlockSpec((1, FFN_CONV - 1, D_FF), lambda i, j: (i, 0, 0)),
        ],
        scratch_shapes=[pltpu.VMEM((tm + 8, D_FF), F32)],
        compiler_params=_params(2),
        name="mix_out_conv_ffn",
    )(x, m1, m2, mod_l, gain, hist, wo1, wo2, wg, wu, cw, wd)


def _pad_cols(w, n):
    return jnp.pad(w, ((0, 0), (0, n - w.shape[1])))


def _lane_row(vec, offset):
    return jnp.pad(vec, (offset, LANES - offset - vec.shape[0])).reshape(1, LANES)


def _prep_even(W, i):
    f = lambda name: W[name][i]
    w_in = f("even_w_in")
    mla_cols = MLA_Q_RANK + MLA_KV_RANK
    kr_w = jnp.pad(w_in[:, mla_cols:mla_cols + MLA_ROPE], ((0, 0), (MLA_NOPE, LANES - MLA_NOPE - MLA_ROPE)))
    w_mla = jnp.concatenate([w_in[:, :mla_cols], kr_w], axis=1).astype(BF16)
    w_rw = w_in[:, mla_cols + MLA_ROPE:].astype(BF16)
    per_q = MLA_NOPE + MLA_ROPE
    w_uq = f("mla_w_uq").reshape(MLA_Q_RANK, MLA_HEADS, per_q)
    w_uq = jnp.pad(w_uq, ((0, 0), (0, 0), (0, HEAD_PAD - per_q))).reshape(MLA_Q_RANK, MLA_HEADS * HEAD_PAD)
    w_ukv = f("mla_w_ukv").reshape(MLA_KV_RANK, MLA_HEADS, MLA_NOPE + MLA_V)
    w_uk = jnp.pad(w_ukv[:, :, :MLA_NOPE], ((0, 0), (0, 0), (0, HEAD_PAD - MLA_NOPE)))
    vv = w_ukv[:, :, MLA_NOPE:]
    v_even = jnp.pad(vv, ((0, 0), (0, 0), (0, HEAD_PAD - MLA_V)))
    v_odd = jnp.pad(vv, ((0, 0), (0, 0), (HEAD_PAD - MLA_V, 0)))
    odd = (jnp.arange(MLA_HEADS) % 2 == 1)[None, :, None]
    w_uv = jnp.where(odd, v_odd, v_even)
    gi = jnp.arange(LANES)
    grp = jnp.where(gi < MLA_NOPE, 0, jnp.where(gi < per_q, 1, 2))
    gmat_q = jnp.where((grp[:, None] == grp[None, :]) & (grp[:, None] < 2),
                       jnp.where(grp[:, None] == 0, 1.0 / MLA_NOPE, 1.0 / MLA_ROPE), 0.0)
    hid = jnp.arange(256) // RWKV_HEAD
    w_out = f("even_w_out").astype(BF16)
    zpad = jnp.zeros((RWKV_HEAD, RWKV_DIM), F32)
    return {
        "w_in": [w_mla, w_rw],
        "g_qlat": f("mla_g_qlat").reshape(1, -1), "g_kvlat": f("mla_g_kvlat").reshape(1, -1),
        "w_uq": w_uq.astype(BF16),
        "g_qhead": jnp.concatenate([f("mla_g_qn"), f("mla_g_qr"), jnp.zeros((HEAD_PAD - per_q,), F32)]).reshape(1, -1),
        "g_kr": _lane_row(f("mla_g_kr"), MLA_NOPE),
        "gmat_q": gmat_q.astype(BF16),
        "w_uk": w_uk.reshape(MLA_KV_RANK, -1).astype(BF16), "w_uv": w_uv.reshape(MLA_KV_RANK, -1).astype(BF16),
        "g_kn": _lane_row(f("mla_g_kn"), 0),
        "mu": f("rwkv_mu").reshape(1, -1), "w0": f("rwkv_w0").reshape(1, -1), "a0": f("rwkv_a0").reshape(1, -1),
        "w2": jnp.concatenate([f("rwkv_w2"), zpad], axis=0).astype(BF16),
        "a2": jnp.concatenate([zpad, f("rwkv_a2")], axis=0).astype(BF16),
        "g2": f("rwkv_g2").astype(BF16),
        "k_k": f("rwkv_k_k").reshape(1, -1), "k_a": f("rwkv_k_a").reshape(1, -1),
        "r_k": f("rwkv_r_k").reshape(1, -1),
        "lnx_g": f("rwkv_lnx_g").reshape(1, -1), "lnx_b": f("rwkv_lnx_b").reshape(1, -1),
        "gsum": (hid[:, None] == hid[None, :]).astype(BF16),
        "w_out1": w_out[:MLA_HEADS * MLA_V], "w_out2": w_out[MLA_HEADS * MLA_V:],
    }


def _prep_odd(W, i):
    f = lambda name: W[name][i]
    w_in = f("odd_w_in")
    c1 = POOL_DIM + GDN_QKV
    c2 = c1 + GDN_DIM
    w_out = f("odd_w_out").astype(BF16)
    return {
        "w_in": [w_in[:, :POOL_DIM].astype(BF16), w_in[:, POOL_DIM:c1].astype(BF16), w_in[:, c1:c2].astype(BF16),
                 _pad_cols(w_in[:, c2:], LANES).astype(BF16)],
        "pool_w": f("pool_w").astype(BF16), "pool_scale": f("pool_scale").reshape(1, -1),
        "conv_w": f("gdn_conv_w"),
        "a_log": _lane_row(f("gdn_a_log"), GDN_HEADS), "dt_bias": _lane_row(f("gdn_dt_bias"), GDN_HEADS),
        "o_g": f("gdn_o_g").reshape(1, -1),
        "sel": (jnp.arange(LANES)[None, :] == (jnp.arange(8)[:, None] + GDN_HEADS)).astype(BF16),
        "w_out1": w_out[:POOL_DIM], "w_out2": w_out[POOL_DIM:],
    }


def _rope_tabs(pos0, t):
    inv = 1.0 / (ROPE_THETA ** (jnp.arange(0, MLA_ROPE, 2, dtype=F32) / MLA_ROPE))
    ang = (pos0 + jnp.arange(t, dtype=jnp.int32)).astype(F32)[:, None] * inv[None, :]
    cos, sin = jnp.cos(ang), jnp.sin(ang)
    z16 = jnp.zeros_like(cos)
    one = jnp.ones((t, MLA_NOPE), F32)
    z64 = jnp.zeros((t, MLA_NOPE), F32)
    z32 = jnp.zeros((t, LANES - MLA_NOPE - MLA_ROPE), F32)
    return (jnp.concatenate([one, cos, cos, z32], axis=1),
            jnp.concatenate([z64, -sin, z16, z32], axis=1),
            jnp.concatenate([z64, z16, sin, z32], axis=1))


def _to_blockdiag(s):
    b = s.shape[0]
    s = s.reshape(b, RWKV_HEADS // 2, 2, RWKV_HEAD, RWKV_HEAD)
    z = jnp.zeros_like(s[:, :, 0])
    top = jnp.concatenate([s[:, :, 0], z], axis=-1)
    bot = jnp.concatenate([z, s[:, :, 1]], axis=-1)
    return jnp.concatenate([top, bot], axis=-2)


def _from_blockdiag(sb):
    b = sb.shape[0]
    s0 = sb[:, :, :RWKV_HEAD, :RWKV_HEAD]
    s1 = sb[:, :, RWKV_HEAD:, RWKV_HEAD:]
    return jnp.stack([s0, s1], axis=2).reshape(b, RWKV_HEADS, RWKV_HEAD, RWKV_HEAD)


def _trunk(x, mod, pos0, caches, W, prep):
    b, t, _ = x.shape
    ckv_c, kpe_c, shift_c, wkv_c, pool_c, conv_c, gdn_c, ffn_c = caches
    tabs = _rope_tabs(pos0, t)
    o_ckv, o_kpe, o_shift, o_wkv, o_pool, o_conv, o_gdn, o_ffn = [], [], [], [], [], [], [], []
    for layer in range(DEPTH):
        i = layer // 2
        wl = prep[layer]
        mod_l = mod[layer]
        if layer % 2 == 0:
            p_mla, rw = _inproj(x, mod_l, W["norm_mix_g"][layer].reshape(1, -1), wl["w_in"])
            q, ckv_new, kpe_new, kpe_blk = _mla_prep(p_mla, tabs, wl)
            if ckv_c is None:
                ckv_all, kblk_all, past = ckv_new, kpe_blk, 0
            else:
                past = ckv_c.shape[2]
                ckv_all = jnp.concatenate([ckv_c[i], ckv_new], axis=1)
                past_blk = jnp.pad(kpe_c[i], ((0, 0), (0, 0), (MLA_NOPE, LANES - MLA_NOPE - MLA_ROPE)))
                kblk_all = jnp.concatenate([past_blk, kpe_blk], axis=1)
            kk, vv = _mla_kv(ckv_all, kblk_all, wl)
            m1 = _attention(q, kk, vv, past)
            m2, sblk = _rwkv(rw, shift_c[i][:, None, :], _to_blockdiag(wkv_c[i]), wl)
            o_ckv.append(ckv_new)
            o_kpe.append(kpe_new)
            o_shift.append(rw[:, -1])
            o_wkv.append(_from_blockdiag(sblk))
        else:
            u, qkv, z, ba = _inproj(x, mod_l, W["norm_mix_g"][layer].reshape(1, -1), wl["w_in"])
            hist16 = jnp.pad(pool_c[i], ((0, 0), (1, 0), (0, 0)))
            m1 = _pool(u, hist16, pos0, wl)
            m2, gs = _gdn(qkv, z, ba, conv_c[i], gdn_c[i], wl)
            o_pool.append(u[:, t - POOL_HIST:])
            o_conv.append(qkv[:, t - (GDN_CONV - 1):])
            o_gdn.append(gs)
        x, fc = _block(x, m1, m2, mod_l, W["norm_ffn_g"][layer].reshape(1, -1), ffn_c[layer],
                       wl["w_out1"], wl["w_out2"], wl["ffn_wg"], wl["ffn_wu"], W["ffn_conv_w"][layer], wl["ffn_wd"])
        o_ffn.append(fc)
    st = jnp.stack
    return x, (st(o_ckv), st(o_kpe), st(o_shift), st(o_wkv), st(o_pool), st(o_conv), st(o_gdn), st(o_ffn))


def _zero_caches(b, dt):
    n_even, n_odd = (DEPTH + 1) // 2, DEPTH // 2
    return (None, None,
            jnp.zeros((n_even, b, RWKV_COLS), dt),
            jnp.zeros((n_even, b, RWKV_HEADS, RWKV_HEAD, RWKV_HEAD), dt),
            jnp.zeros((n_odd, b, POOL_HIST, POOL_DIM), dt),
            jnp.zeros((n_odd, b, GDN_CONV - 1, GDN_QKV), dt),
            jnp.zeros((n_odd, b, GDN_HEADS, GDN_DK, GDN_DV), dt),
            jnp.zeros((DEPTH, b, FFN_CONV - 1, D_FF), dt))


def _run(x_prompt, x_sample, caches_s, c_prompt, c_sample, W):
    bp = x_prompt.shape[0]
    mod = _ada(jnp.concatenate([c_prompt, c_sample], axis=0), W["ada_w"], W["ada_b"])
    mod = mod.reshape(DEPTH, -1, 6, D_MODEL)
    prep = [(_prep_even if l % 2 == 0 else _prep_odd)(W, l // 2) for l in range(DEPTH)]
    for l in range(DEPTH):
        prep[l]["ffn_wg"] = W["ffn_w_gate"][l].astype(BF16)
        prep[l]["ffn_wu"] = W["ffn_w_up"][l].astype(BF16)
        prep[l]["ffn_wd"] = W["ffn_w_down"][l].astype(BF16)
    y_p, st_p = _trunk(x_prompt, mod[:, :bp], 0, _zero_caches(bp, x_prompt.dtype), W, prep)
    past_len = caches_s[0].shape[2]
    y_s, st_s = _trunk(x_sample, mod[:, bp:], past_len, caches_s, W, prep)
    return (y_p, y_s) + tuple(st_p) + tuple(st_s)


def kernel(x_prompt, x_sample, cache_mla_ckv, cache_mla_kpe, state_rwkv_shift, state_rwkv_wkv, state_pool,
           state_gdn_conv, state_gdn, state_ffn_conv, c_prompt, c_sample, ada_w, ada_b, norm_mix_g, norm_ffn_g,
           even_w_in, mla_g_qlat, mla_g_kvlat, mla_w_uq, mla_w_ukv, mla_g_qn, mla_g_qr, mla_g_kn, mla_g_kr,
           rwkv_mu, rwkv_w0, rwkv_w2, rwkv_a0, rwkv_a2, rwkv_g2, rwkv_k_k, rwkv_k_a, rwkv_r_k, rwkv_lnx_g,
           rwkv_lnx_b, even_w_out, odd_w_in, pool_w, pool_scale, gdn_conv_w, gdn_a_log, gdn_dt_bias, gdn_o_g,
           odd_w_out, ffn_w_gate, ffn_w_up, ffn_conv_w, ffn_w_down):
    W = {
        "ada_w": ada_w, "ada_b": ada_b, "norm_mix_g": norm_mix_g, "norm_ffn_g": norm_ffn_g,
        "even_w_in": even_w_in, "mla_g_qlat": mla_g_qlat, "mla_g_kvlat": mla_g_kvlat,
        "mla_w_uq": mla_w_uq, "mla_w_ukv": mla_w_ukv, "mla_g_qn": mla_g_qn, "mla_g_qr": mla_g_qr,
        "mla_g_kn": mla_g_kn, "mla_g_kr": mla_g_kr,
        "rwkv_mu": rwkv_mu, "rwkv_w0": rwkv_w0, "rwkv_w2": rwkv_w2, "rwkv_a0": rwkv_a0, "rwkv_a2": rwkv_a2,
        "rwkv_g2": rwkv_g2, "rwkv_k_k": rwkv_k_k, "rwkv_k_a": rwkv_k_a, "rwkv_r_k": rwkv_r_k,
        "rwkv_lnx_g": rwkv_lnx_g, "rwkv_lnx_b": rwkv_lnx_b, "even_w_out": even_w_out,
        "odd_w_in": odd_w_in, "pool_w": pool_w, "pool_scale": pool_scale, "gdn_conv_w": gdn_conv_w,
        "gdn_a_log": gdn_a_log, "gdn_dt_bias": gdn_dt_bias, "gdn_o_g": gdn_o_g, "odd_w_out": odd_w_out,
        "ffn_w_gate": ffn_w_gate, "ffn_w_up": ffn_w_up, "ffn_conv_w": ffn_conv_w, "ffn_w_down": ffn_w_down,
    }
    caches_s = (cache_mla_ckv, cache_mla_kpe, state_rwkv_shift, state_rwkv_wkv, state_pool, state_gdn_conv,
                state_gdn, state_ffn_conv)
    return _run(x_prompt, x_sample, caches_s, c_prompt, c_sample, W)
```

```python
import functools
import math

import jax
import jax.numpy as jnp
from jax import lax
from jax.experimental import pallas as pl
from jax.experimental.pallas import tpu as pltpu

F32 = jnp.float32
BF16 = jnp.bfloat16

D_MODEL = 1024
DEPTH = 4
CHUNK = 64
EPS = 1e-6
L2_EPS = 1e-6
NEG_INF = -1e30

MLA_HEADS = 8
MLA_NOPE = 64
MLA_ROPE = 32
MLA_V = 64
MLA_Q_RANK = 256
MLA_KV_RANK = 128
MLA_SCALE = (MLA_NOPE + MLA_ROPE) ** -0.5
ROPE_THETA = 10000.0

RWKV_HEADS = 8
RWKV_HEAD = 64
RWKV_DIM = 512
RWKV_COLS = 1792
RWKV_LN_EPS = 64e-5

POOL_DIM = 512
POOL_WINDOWS = (2, 4, 8, 16)
POOL_HIST = 15

GDN_HEADS = 4
GDN_DK = 128
GDN_DV = 128
GDN_CONV = 4
GDN_QKV = 1536
GDN_DIM = 512

D_FF = 2816
FFN_CONV = 3

LANES = 128
HEAD_PAD = 128
VMEM_LIMIT = 56 * 1024 * 1024


def _dot(a, b):
    return jnp.dot(a, b, preferred_element_type=F32)


def _dot_nt(a, b):
    return lax.dot_general(a, b, (((1,), (1,)), ((), ())), preferred_element_type=F32)


def _dot_tn(a, b):
    return lax.dot_general(a, b, (((0,), (0,)), ((), ())), preferred_element_type=F32)


def _split(x, n):
    parts = []
    r = x
    for i in range(n):
        p = r.astype(BF16)
        parts.append(p)
        if i + 1 < n:
            r = r - p.astype(F32)
    return parts


def _sum_left(mat01, x, n):
    out = None
    for p in _split(x, n):
        d = _dot(mat01, p)
        out = d if out is None else out + d
    return out


def _sum_right(x, mat01, n):
    out = None
    for p in _split(x, n):
        d = _dot(p, mat01)
        out = d if out is None else out + d
    return out


def _sigmoid(x):
    return 1.0 / (1.0 + jnp.exp(-x))


def _silu(x):
    return x * _sigmoid(x)


def _softplus(x):
    return jnp.maximum(x, 0.0) + jnp.log(1.0 + jnp.exp(-jnp.abs(x)))


def _rms(x, gain, eps=EPS):
    return x * lax.rsqrt(jnp.mean(x * x, axis=-1, keepdims=True) + eps) * gain


def _modulate(x, gain, shift, scale):
    return _rms(x, gain) * (1.0 + scale) + shift


def _solve_unit_lower(ns, rhss, steps, refine):
    size = ns[0].shape[0]
    ri = lax.broadcasted_iota(jnp.int32, (size, size), 0)
    ci = lax.broadcasted_iota(jnp.int32, (size, size), 1)
    eye = jnp.where(ri == ci, 1.0, 0.0)
    xs = [eye + n for n in ns]
    pbs = [n.astype(BF16) for n in ns]
    pbs = [_dot(p, p).astype(BF16) for p in pbs]
    for _ in range(steps - 1):
        xs = [x + _dot(x.astype(BF16), p) for x, p in zip(xs, pbs)]
        pbs = [_dot(p, p).astype(BF16) for p in pbs]
    xbs = [(x + _dot(x.astype(BF16), p)).astype(BF16) for x, p in zip(xs, pbs)]
    us = [_dot(xb, r.astype(BF16)) for xb, r in zip(xbs, rhss)]
    if not refine:
        return us
    nbs = [n.astype(BF16) for n in ns]
    usp = [_split(u, 2) for u in us]
    nus = [_dot(nb, uh) for nb, (uh, _) in zip(nbs, usp)]
    nus = [nu + _dot(nb, ul) for nu, nb, (_, ul) in zip(nus, nbs, usp)]
    return [u + _dot(xb, (r - u + nu).astype(BF16)) for u, xb, r, nu in zip(us, xbs, rhss, nus)]


def _const_spec(shape):
    nd = len(shape)
    return pl.BlockSpec(shape, lambda *_: (0,) * nd, pipeline_mode=pl.Buffered(1))


def _params(n_axes):
    return pltpu.CompilerParams(dimension_semantics=("arbitrary",) * n_axes, vmem_limit_bytes=VMEM_LIMIT)


def _time_tile(t, pref):
    return pref if t % pref == 0 else t


def _ada_kernel(c_ref, w_ref, b_ref, o_ref):
    s = _silu(c_ref[...]).astype(BF16)
    o_ref[0] = _dot(s, w_ref[0].astype(BF16)) + b_ref[0]


def _ada(c_all, ada_w, ada_b):
    n = c_all.shape[0]
    tn = 1536
    return pl.pallas_call(
        _ada_kernel,
        out_shape=jax.ShapeDtypeStruct((DEPTH, n, 6 * D_MODEL), F32),
        grid=(DEPTH, 6 * D_MODEL // tn),
        in_specs=[
            pl.BlockSpec((n, D_MODEL), lambda l, j: (0, 0)),
            pl.BlockSpec((1, D_MODEL, tn), lambda l, j: (l, 0, j)),
            pl.BlockSpec((1, 1, tn), lambda l, j: (l, 0, j)),
        ],
        out_specs=pl.BlockSpec((1, n, tn), lambda l, j: (l, 0, j)),
        compiler_params=_params(2),
        name="ada_mod",
    )(c_all, ada_w, ada_b.reshape(DEPTH, 1, 6 * D_MODEL))


def _inproj_kernel(x_ref, mod_ref, g_ref, *refs, nseg):
    m = mod_ref[0]
    h = _modulate(x_ref[0], g_ref[...], m[0:1], m[1:2]).astype(BF16)
    for w_ref, o_ref in zip(refs[:nseg], refs[nseg:]):
        o_ref[0] = _dot(h, w_ref[...])


def _inproj(x, mod_l, gain, w_segs):
    b, t, d = x.shape
    tm = _time_tile(t, 512)
    nseg = len(w_segs)
    return pl.pallas_call(
        functools.partial(_inproj_kernel, nseg=nseg),
        out_shape=[jax.ShapeDtypeStruct((b, t, w.shape[1]), F32) for w in w_segs],
        grid=(b, t // tm),
        in_specs=[
            pl.BlockSpec((1, tm, d), lambda i, j: (i, j, 0)),
            pl.BlockSpec((1, 6, d), lambda i, j: (i, 0, 0)),
            _const_spec((1, d)),
        ] + [_const_spec(w.shape) for w in w_segs],
        out_specs=[pl.BlockSpec((1, tm, w.shape[1]), lambda i, j: (i, j, 0)) for w in w_segs],
        compiler_params=_params(2),
        name="in_proj",
    )(x, mod_l, gain, *w_segs)


def _rope(blk, c, sm, sp):
    return blk * c + pltpu.roll(blk, LANES - 16, 1) * sm + pltpu.roll(blk, 16, 1) * sp


def _mla_prep_kernel(p_ref, c_ref, sm_ref, sp_ref, gq_ref, gkv_ref, wuq_ref, ghead_ref, gkr_ref, gmat_ref,
                     q_ref, ckv_ref, kpe_ref, kblk_ref):
    p = p_ref[0]
    c, sm, sp = c_ref[...], sm_ref[...], sp_ref[...]
    qn = _rms(p[:, :MLA_Q_RANK], gq_ref[...]).astype(BF16)
    q = _dot(qn, wuq_ref[...])
    gmat = gmat_ref[...]
    for h in range(MLA_HEADS):
        hb = q[:, h * HEAD_PAD:(h + 1) * HEAD_PAD]
        ms = _sum_right(hb * hb, gmat, 2)
        hb = hb * lax.rsqrt(ms + EPS) * ghead_ref[...]
        q_ref[0, h] = _rope(hb, c, sm, sp).astype(BF16)
    ckv_ref[0] = _rms(p[:, MLA_Q_RANK:MLA_Q_RANK + MLA_KV_RANK], gkv_ref[...])
    kr = p[:, MLA_Q_RANK + MLA_KV_RANK:]
    ms = jnp.sum(kr * kr, axis=-1, keepdims=True) * (1.0 / MLA_ROPE)
    kb = _rope(kr * lax.rsqrt(ms + EPS) * gkr_ref[...], c, sm, sp)
    kblk_ref[0] = kb
    kpe_ref[0] = kb[:, MLA_NOPE:MLA_NOPE + MLA_ROPE]


def _mla_prep(p_mla, tabs, wl):
    b, t, _ = p_mla.shape
    tm = _time_tile(t, 512)
    tab_spec = pl.BlockSpec((tm, LANES), lambda i, j: (j, 0))
    return pl.pallas_call(
        _mla_prep_kernel,
        out_shape=[
            jax.ShapeDtypeStruct((b, MLA_HEADS, t, HEAD_PAD), BF16),
            jax.ShapeDtypeStruct((b, t, MLA_KV_RANK), F32),
            jax.ShapeDtypeStruct((b, t, MLA_ROPE), F32),
            jax.ShapeDtypeStruct((b, t, LANES), F32),
        ],
        grid=(b, t // tm),
        in_specs=[
            pl.BlockSpec((1, tm, 512), lambda i, j: (i, j, 0)),
            tab_spec, tab_spec, tab_spec,
            _const_spec((1, MLA_Q_RANK)), _const_spec((1, MLA_KV_RANK)),
            _const_spec((MLA_Q_RANK, MLA_HEADS * HEAD_PAD)),
            _const_spec((1, LANES)), _const_spec((1, LANES)), _const_spec((LANES, LANES)),
        ],
        out_specs=[
            pl.BlockSpec((1, MLA_HEADS, tm, HEAD_PAD), lambda i, j: (i, 0, j, 0)),
            pl.BlockSpec((1, tm, MLA_KV_RANK), lambda i, j: (i, j, 0)),
            pl.BlockSpec((1, tm, MLA_ROPE), lambda i, j: (i, j, 0)),
            pl.BlockSpec((1, tm, LANES), lambda i, j: (i, j, 0)),
        ],
        compiler_params=_params(2),
        name="mla_prep",
    )(p_mla, tabs[0], tabs[1], tabs[2], wl["g_qlat"], wl["g_kvlat"], wl["w_uq"], wl["g_qhead"], wl["g_kr"],
      wl["gmat_q"])


def _ones_lane(h):
    return MLA_V if h % 2 == 0 else 0


def _mla_kv_kernel(ckv_ref, kpe_ref, wuk_ref, wuv_ref, gkn_ref, k_ref, v_ref):
    cb = ckv_ref[0].astype(BF16)
    kf = _dot(cb, wuk_ref[...])
    vf = _dot(cb, wuv_ref[...])
    kpe = kpe_ref[0]
    lane = lax.broadcasted_iota(jnp.int32, (1, LANES), 1)
    for h in range(MLA_HEADS):
        kb = kf[:, h * HEAD_PAD:(h + 1) * HEAD_PAD]
        ms = jnp.sum(kb * kb, axis=-1, keepdims=True) * (1.0 / MLA_NOPE)
        k_ref[0, h] = (kb * lax.rsqrt(ms + EPS) * gkn_ref[...] + kpe).astype(BF16)
        ones = jnp.where(lane == _ones_lane(h), 1.0, 0.0)
        v_ref[0, h] = (vf[:, h * HEAD_PAD:(h + 1) * HEAD_PAD] + ones).astype(BF16)


def _mla_kv(ckv_all, kpe_blk, wl):
    b, l, _ = ckv_all.shape
    tl = _time_tile(l, 512)
    return pl.pallas_call(
        _mla_kv_kernel,
        out_shape=[jax.ShapeDtypeStruct((b, MLA_HEADS, l, HEAD_PAD), BF16)] * 2,
        grid=(b, l // tl),
        in_specs=[
            pl.BlockSpec((1, tl, MLA_KV_RANK), lambda i, j: (i, j, 0)),
            pl.BlockSpec((1, tl, LANES), lambda i, j: (i, j, 0)),
            _const_spec((MLA_KV_RANK, MLA_HEADS * HEAD_PAD)),
            _const_spec((MLA_KV_RANK, MLA_HEADS * HEAD_PAD)),
            _const_spec((1, LANES)),
        ],
        out_specs=[pl.BlockSpec((1, MLA_HEADS, tl, HEAD_PAD), lambda i, j: (i, 0, j, 0))] * 2,
        compiler_params=_params(2),
        name="mla_kv",
    )(ckv_all, kpe_blk, wl["w_uk"], wl["w_uv"], wl["g_kn"])


def _attn_kernel(q_ref, k_ref, v_ref, o_ref, m_scr, acc_scr, *, tq, tk, past_len):
    d0 = pl.multiple_of(past_len + pl.program_id(1) * tq, tq)
    nprefix = d0 // tk
    qc = lax.broadcasted_iota(jnp.int32, (tq, tq), 0) // CHUNK
    kc = lax.broadcasted_iota(jnp.int32, (tq, tq), 1) // CHUNK
    mask = kc <= qc
    heads = range(MLA_HEADS)

    def step(r0, n, first):
        ss = [_dot_nt(q_ref[0, h], k_ref[0, h, pl.ds(r0, n), :]) for h in heads]
        for h in heads:
            s = ss[h] * MLA_SCALE
            if first:
                s = jnp.where(mask, s, NEG_INF)
            mx = jnp.max(s, axis=-1, keepdims=True)
            m_old = m_scr[h]
            mn = jnp.broadcast_to(mx, (tq, LANES)) if first else jnp.maximum(m_old, mx)
            m_scr[h] = mn
            mw = jnp.concatenate([mn] * (n // LANES), axis=1) if n >= LANES else mn[:, :n]
            pv = _dot(jnp.exp(s - mw).astype(BF16), v_ref[0, h, pl.ds(r0, n), :])
            acc_scr[h] = pv if first else jnp.exp(m_old - mn) * acc_scr[h] + pv

    step(d0, tq, True)

    def body(j, carry):
        step(pl.multiple_of(j * tk, tk), tk, False)
        return carry

    lax.fori_loop(0, nprefix, body, 0)
    lane = lax.broadcasted_iota(jnp.int32, (1, LANES), 1)
    outs = []
    for h in heads:
        acc = acc_scr[h]
        c = _ones_lane(h)
        own = (lane < MLA_V) if h % 2 == 0 else (lane >= MLA_V)
        outs.append(jnp.where(own, acc / acc[:, c:c + 1], 0.0))
    for j in range(MLA_HEADS // 2):
        o_ref[0, :, j * LANES:(j + 1) * LANES] = outs[2 * j] + outs[2 * j + 1]


def _attention(q, k, v, past_len):
    b, hh, t, _ = q.shape
    l = k.shape[2]
    tq = _time_tile(t, 256)
    tk = 256
    assert past_len % tk == 0 and (tq % tk == 0 or past_len + tq == l) and l == past_len + t
    return pl.pallas_call(
        functools.partial(_attn_kernel, tq=tq, tk=tk, past_len=past_len),
        out_shape=jax.ShapeDtypeStruct((b, t, MLA_HEADS * MLA_V), F32),
        grid=(b, t // tq),
        in_specs=[
            pl.BlockSpec((1, hh, tq, HEAD_PAD), lambda i, j: (i, 0, j, 0)),
            pl.BlockSpec((1, hh, l, HEAD_PAD), lambda i, j: (i, 0, 0, 0)),
            pl.BlockSpec((1, hh, l, HEAD_PAD), lambda i, j: (i, 0, 0, 0)),
        ],
        out_specs=pl.BlockSpec((1, tq, MLA_HEADS * MLA_V), lambda i, j: (i, j, 0)),
        scratch_shapes=[pltpu.VMEM((MLA_HEADS, tq, LANES), F32), pltpu.VMEM((MLA_HEADS, tq, LANES), F32)],
        compiler_params=_params(2),
        name="mla_attn",
    )(q, k, v)


def _rwkv_kernel(rw_ref, sh_ref, s0_ref, mu_ref, w0_ref, w2_ref, a0_ref, a2_ref, g2_ref, kkw_ref, ka_ref,
                 rk_ref, lng_ref, lnb_ref, gsum_ref, y_ref, s_ref, xs_scr, st_scr, *, tt, cc):
    t = pl.program_id(1)

    @pl.when(t == 0)
    def _():
        xs_scr[7:8, :] = sh_ref[0]
        st_scr[...] = s0_ref[0]

    rw = rw_ref[0]
    xs_scr[8:8 + tt, :] = rw
    prev = xs_scr[7:7 + tt, :]
    xm = rw + (prev - rw) * mu_ref[...]
    xs_scr[7:8, :] = rw[tt - 1:tt, :]

    r = xm[:, 0:RWKV_DIM]
    k = xm[:, RWKV_DIM:2 * RWKV_DIM]
    v = xm[:, 2 * RWKV_DIM:3 * RWKV_DIM]
    lw = xm[:, 3 * RWKV_DIM:3 * RWKV_DIM + LANES]
    dg = xm[:, 3 * RWKV_DIM + LANES:]
    zw = w0_ref[...] + _dot(jnp.tanh(lw).astype(BF16), w2_ref[...])
    ld = -jnp.exp(-_softplus(-zw) - 0.5)
    a = _sigmoid(a0_ref[...] + _dot(lw.astype(BF16), a2_ref[...]))
    g = _dot(_sigmoid(dg).astype(BF16), g2_ref[...])

    gmat = gsum_ref[...]

    def gsum(x):
        return jnp.concatenate([_sum_right(x[:, :256], gmat, 2), _sum_right(x[:, 256:], gmat, 2)], axis=-1)

    kkr = k * kkw_ref[...]
    kk = kkr * lax.rsqrt(gsum(kkr * kkr) + L2_EPS)
    k2 = k * (1.0 + (a - 1.0) * ka_ref[...])
    kka = kk * a
    bonus = gsum(r * k2 * rk_ref[...]) * v

    c2 = 2 * cc
    nch = tt // cc
    npair = RWKV_HEADS // 2
    ri = lax.broadcasted_iota(jnp.int32, (tt, tt), 0)
    ci = lax.broadcasted_iota(jnp.int32, (tt, tt), 1)
    tri = jnp.where(ci <= ri, jnp.where(ci // cc == ri // cc, 1.0, 0.0), 0.0).astype(BF16)
    rl = lax.broadcasted_iota(jnp.int32, (c2, 2 * c2), 0) % cc
    cl = lax.broadcasted_iota(jnp.int32, (c2, 2 * c2), 1) % cc
    strict = cl < rl
    incl = cl <= rl
    low = lax.broadcasted_iota(jnp.int32, (1, LANES), 1) < RWKV_HEAD

    cs = _sum_left(tri, ld, 3)
    tots = [cs[(c + 1) * cc - 1:(c + 1) * cc, :] for c in range(nch)]
    tot = jnp.concatenate([jnp.broadcast_to(tc, (cc, RWKV_DIM)) for tc in tots], axis=0) if nch > 1 else tots[0]
    pinv = jnp.exp(-cs)
    pend = jnp.exp(tot - cs)
    at = -(kk * jnp.exp(cs - ld))
    rt = r * jnp.exp(cs)
    bt = kka * pinv
    kt = k2 * pinv
    bh = kka * pend
    kh = k2 * pend

    inst = [(c, j) for c in range(nch) for j in range(npair)]

    def blk(x, c, j):
        xp = x[c * cc:(c + 1) * cc, j * LANES:(j + 1) * LANES]
        return jnp.concatenate([jnp.where(low, xp, 0.0), jnp.where(low, 0.0, xp)], axis=0)

    a_b = [blk(at, c, j) for c, j in inst]
    r_b = [blk(rt, c, j) for c, j in inst]
    v_b = [blk(v, c, j) for c, j in inst]
    bk_b = [jnp.concatenate([blk(bh, c, j), blk(kh, c, j)], axis=0).astype(BF16) for c, j in inst]
    lhs = [jnp.concatenate([a, rr], axis=0).astype(BF16) for a, rr in zip(a_b, r_b)]
    rhs = [jnp.concatenate([blk(bt, c, j), blk(kt, c, j)], axis=0).astype(BF16) for c, j in inst]
    mm = [_dot_nt(l, rr) for l, rr in zip(lhs, rhs)]
    mt = [jnp.where(strict, m[:c2], 0.0) for m in mm]
    mb = [jnp.where(incl, m[c2:], 0.0).astype(BF16) for m in mm]
    akv = [_dot(m.astype(BF16), jnp.concatenate([jnp.zeros_like(vb), vb], axis=0).astype(BF16))
           for m, vb in zip(mt, v_b)]
    zs = _solve_unit_lower([m[:, :c2] for m in mt], [jnp.concatenate([a, w], axis=1) for a, w in zip(a_b, akv)],
                           int(math.log2(cc)) - 1, refine=False)
    ahat = [z[:, :LANES].astype(BF16) for z in zs]
    wv = [jnp.concatenate([z[:, LANES:], vb], axis=0).astype(BF16) for z, vb in zip(zs, v_b)]
    rhat = [(rr + _dot(m[:, :c2], ah)).astype(BF16) for rr, m, ah in zip(r_b, mb, ahat)]
    yc = [_dot(m, w) for m, w in zip(mb, wv)]
    phi = [_dot_tn(ah, bk[:c2]).astype(BF16) for ah, bk in zip(ahat, bk_b)]
    gam = [_dot_tn(w, bk) for w, bk in zip(wv, bk_b)]

    sb = [st_scr[j] for j in range(npair)]
    ys = []
    for c in range(nch):
        ptot = jnp.exp(tots[c])
        sbb = [s.astype(BF16) for s in sb]
        yb = [_dot_nt(rhat[c * npair + j], sbb[j]) + yc[c * npair + j] for j in range(npair)]
        sb = [sb[j] * ptot[:, j * LANES:(j + 1) * LANES] + _dot(sbb[j], phi[c * npair + j]) + gam[c * npair + j]
              for j in range(npair)]
        ys.append(jnp.concatenate([yy[:cc] + yy[cc:] for yy in yb], axis=-1))
    for j in range(npair):
        st_scr[j] = sb[j]
    y = jnp.concatenate(ys, axis=0) if nch > 1 else ys[0]

    mu = gsum(y) * (1.0 / RWKV_HEAD)
    dy = y - mu
    var = gsum(dy * dy) * (1.0 / RWKV_HEAD)
    yn = dy * lax.rsqrt(var + RWKV_LN_EPS) * lng_ref[...] + lnb_ref[...]
    y_ref[0] = (yn + bonus) * g
    s_ref[0] = st_scr[...]


def _rwkv(rw, shift, sblk, wl):
    b, t, _ = rw.shape
    cc = min(t, CHUNK)
    tt = _time_tile(t, 256)
    row = lambda n: _const_spec((1, n))
    return pl.pallas_call(
        functools.partial(_rwkv_kernel, tt=tt, cc=cc),
        out_shape=[
            jax.ShapeDtypeStruct((b, t, RWKV_DIM), F32),
            jax.ShapeDtypeStruct((b, RWKV_HEADS // 2, LANES, LANES), F32),
        ],
        grid=(b, t // tt),
        in_specs=[
            pl.BlockSpec((1, tt, RWKV_COLS), lambda i, j: (i, j, 0)),
            pl.BlockSpec((1, 1, RWKV_COLS), lambda i, j: (i, 0, 0)),
            pl.BlockSpec((1, RWKV_HEADS // 2, LANES, LANES), lambda i, j: (i, 0, 0, 0)),
            row(RWKV_COLS), row(RWKV_DIM), _const_spec((LANES, RWKV_DIM)), row(RWKV_DIM),
            _const_spec((LANES, RWKV_DIM)), _const_spec((LANES, RWKV_DIM)),
            row(RWKV_DIM), row(RWKV_DIM), row(RWKV_DIM), row(RWKV_DIM), row(RWKV_DIM),
            _const_spec((256, 256)),
        ],
        out_specs=[
            pl.BlockSpec((1, tt, RWKV_DIM), lambda i, j: (i, j, 0)),
            pl.BlockSpec((1, RWKV_HEADS // 2, LANES, LANES), lambda i, j: (i, 0, 0, 0)),
        ],
        scratch_shapes=[
            pltpu.VMEM((tt + 8, RWKV_COLS), F32),
            pltpu.VMEM((RWKV_HEADS // 2, LANES, LANES), F32),
        ],
        compiler_params=_params(2),
        name="rwkv7",
    )(rw, shift, sblk, wl["mu"], wl["w0"], wl["w2"], wl["a0"], wl["a2"], wl["g2"], wl["k_k"], wl["k_a"],
      wl["r_k"], wl["lnx_g"], wl["lnx_b"], wl["gsum"])


def _pool_kernel(u_ref, hist_ref, pw_ref, ps_ref, o_ref, scr, *, tt, pos0):
    t = pl.program_id(1)

    @pl.when(t == 0)
    def _():
        scr[0:16, :] = hist_ref[0]

    u = u_ref[0]
    scr[16:16 + tt, :] = u
    pos = pos0 + t * tt + lax.broadcasted_iota(jnp.int32, (tt, 1), 0)
    for gi, w in enumerate(POOL_WINDOWS):
        ls = slice(gi * LANES, (gi + 1) * LANES)
        ug = u[:, ls]
        s = ug
        for d in range(1, w):
            s = s + scr[16 - d:16 - d + tt, ls]
        cnt = jnp.minimum(w, pos + 1).astype(F32)
        diff = s / cnt - ug
        o_ref[0, :, ls] = _dot(diff.astype(BF16), pw_ref[gi]) * ps_ref[:, ls]
    scr[0:16, :] = scr[tt:tt + 16, :]


def _pool(u, hist16, pos0, wl):
    b, t, _ = u.shape
    tt = _time_tile(t, 256)
    return pl.pallas_call(
        functools.partial(_pool_kernel, tt=tt, pos0=pos0),
        out_shape=jax.ShapeDtypeStruct((b, t, POOL_DIM), F32),
        grid=(b, t // tt),
        in_specs=[
            pl.BlockSpec((1, tt, POOL_DIM), lambda i, j: (i, j, 0)),
            pl.BlockSpec((1, 16, POOL_DIM), lambda i, j: (i, 0, 0)),
            _const_spec((len(POOL_WINDOWS), LANES, LANES)),
            _const_spec((1, POOL_DIM)),
        ],
        out_specs=pl.BlockSpec((1, tt, POOL_DIM), lambda i, j: (i, j, 0)),
        scratch_shapes=[pltpu.VMEM((tt + 16, POOL_DIM), F32)],
        compiler_params=_params(2),
        name="pool_mix",
    )(u, hist16, wl["pool_w"], wl["pool_scale"])


def _gdn_kernel(qkv_ref, z_ref, ba_ref, hist_ref, s0_ref, cw_ref, alog_ref, dtb_ref, og_ref, sel_ref,
                o_ref, s_ref, x_scr, st_scr, *, tt, cc):
    t = pl.program_id(1)

    @pl.when(t == 0)
    def _():
        x_scr[5:8, :] = hist_ref[0]
        st_scr[...] = s0_ref[0]

    x = qkv_ref[0]
    x_scr[8:8 + tt, :] = x
    cw = cw_ref[...]
    xc = cw[3:4] * x + cw[2:3] * x_scr[7:7 + tt, :] + cw[1:2] * x_scr[6:6 + tt, :] + cw[0:1] * x_scr[5:5 + tt, :]
    x_scr[5:8, :] = x[tt - 3:tt, :]
    xc = _silu(xc)
    ba = ba_ref[0]
    beta_all = _sigmoid(ba)
    g_all = -jnp.exp(alog_ref[...]) * _softplus(ba + dtb_ref[...])

    nch = tt // cc
    ri = lax.broadcasted_iota(jnp.int32, (cc, cc), 0)
    ci = lax.broadcasted_iota(jnp.int32, (cc, cc), 1)
    incl = ci <= ri
    strict = ci < ri
    rt_ = lax.broadcasted_iota(jnp.int32, (tt, tt), 0)
    ct_ = lax.broadcasted_iota(jnp.int32, (tt, tt), 1)
    tri = jnp.where(ct_ <= rt_, jnp.where(ct_ // cc == rt_ // cc, 1.0, 0.0), 0.0).astype(BF16)
    sel = sel_ref[...]
    zz = z_ref[0]

    gcs = _sum_left(tri, g_all, 3)
    grows = []
    for c in range(nch):
        parts = _split(gcs[c * cc:(c + 1) * cc], 3)
        grows.append(_dot_nt(sel, parts[0]) + _dot_nt(sel, parts[1]) + _dot_nt(sel, parts[2]))

    qn, kn = [], []
    for h in range(GDN_HEADS):
        qh = xc[:, h * GDN_DK:(h + 1) * GDN_DK]
        kh = xc[:, (GDN_HEADS + h) * GDN_DK:(GDN_HEADS + h + 1) * GDN_DK]
        qn.append(qh * lax.rsqrt(jnp.sum(qh * qh, axis=-1, keepdims=True) + L2_EPS) * (GDN_DK ** -0.5))
        kn.append(kh * lax.rsqrt(jnp.sum(kh * kh, axis=-1, keepdims=True) + L2_EPS))

    inst = [(c, h) for c in range(nch) for h in range(GDN_HEADS)]
    v0 = 2 * GDN_HEADS * GDN_DK

    qs = [qn[h][c * cc:(c + 1) * cc] for c, h in inst]
    ks = [kn[h][c * cc:(c + 1) * cc] for c, h in inst]
    vs = [xc[c * cc:(c + 1) * cc, v0 + h * GDN_DV:v0 + (h + 1) * GDN_DV] for c, h in inst]
    betas = [beta_all[c * cc:(c + 1) * cc, h:h + 1] for c, h in inst]
    gcols = [gcs[c * cc:(c + 1) * cc, GDN_HEADS + h:GDN_HEADS + h + 1] for c, h in inst]
    glasts = [g[cc - 1:cc, :] for g in gcols]
    egs = [jnp.exp(g) for g in gcols]
    decs = [jnp.where(incl, jnp.exp(jnp.where(incl, g - grows[c][h:h + 1, :], 0.0)), 0.0)
            for g, (c, h) in zip(gcols, inst)]
    grams = [_dot_nt(jnp.concatenate([k_, q_], axis=0).astype(BF16), k_.astype(BF16)) for k_, q_ in zip(ks, qs)]
    n_mats = [-(b_ * gm[:cc] * jnp.where(strict, d_, 0.0)) for b_, gm, d_ in zip(betas, grams, decs)]
    qks = [(gm[cc:] * d_).astype(BF16) for gm, d_ in zip(grams, decs)]
    zs = _solve_unit_lower(n_mats, [jnp.concatenate([b_ * e_ * k_, b_ * v_], axis=1)
                                    for b_, e_, k_, v_ in zip(betas, egs, ks, vs)], int(math.log2(cc)) - 1,
                           refine=True)
    zb = [z.astype(BF16) for z in zs]
    qkz = [_dot(qk, z) for qk, z in zip(qks, zb)]
    qhat = [(e_ * q_ - t_[:, :GDN_DK]).astype(BF16) for e_, q_, t_ in zip(egs, qs, qkz)]
    pg = [_dot_tn((k_ * jnp.exp(gl - g)).astype(BF16), z) for k_, gl, g, z in zip(ks, glasts, gcols, zb)]

    sts = [st_scr[h] for h in range(GDN_HEADS)]
    os_ = []
    for c in range(nch):
        i0 = c * GDN_HEADS
        stb = [s_.astype(BF16) for s_ in sts]
        oo = [_dot(qhat[i0 + h], stb[h]) + qkz[i0 + h][:, GDN_DK:] for h in range(GDN_HEADS)]
        sts = [jnp.exp(glasts[i0 + h]) * sts[h] - _dot(pg[i0 + h][:, :GDN_DK].astype(BF16), stb[h])
               + pg[i0 + h][:, GDN_DK:] for h in range(GDN_HEADS)]
        os_.append(jnp.concatenate(
            [_rms(oo[h], og_ref[...]) * _silu(zz[c * cc:(c + 1) * cc, h * GDN_DV:(h + 1) * GDN_DV])
             for h in range(GDN_HEADS)], axis=-1))
    for h in range(GDN_HEADS):
        st_scr[h] = sts[h]
    o_ref[0] = jnp.concatenate(os_, axis=0) if nch > 1 else os_[0]
    s_ref[0] = st_scr[...]


def _gdn(qkv, z, ba, conv_hist, state, wl):
    b, t, _ = qkv.shape
    cc = min(t, CHUNK)
    tt = _time_tile(t, 256)
    return pl.pallas_call(
        functools.partial(_gdn_kernel, tt=tt, cc=cc),
        out_shape=[
            jax.ShapeDtypeStruct((b, t, GDN_DIM), F32),
            jax.ShapeDtypeStruct((b, GDN_HEADS, GDN_DK, GDN_DV), F32),
        ],
        grid=(b, t // tt),
        in_specs=[
            pl.BlockSpec((1, tt, GDN_QKV), lambda i, j: (i, j, 0)),
            pl.BlockSpec((1, tt, GDN_DIM), lambda i, j: (i, j, 0)),
            pl.BlockSpec((1, tt, LANES), lambda i, j: (i, j, 0)),
            pl.BlockSpec((1, GDN_CONV - 1, GDN_QKV), lambda i, j: (i, 0, 0)),
            pl.BlockSpec((1, GDN_HEADS, GDN_DK, GDN_DV), lambda i, j: (i, 0, 0, 0)),
            _const_spec((GDN_CONV, GDN_QKV)),
            _const_spec((1, LANES)), _const_spec((1, LANES)), _const_spec((1, GDN_DV)),
            _const_spec((8, LANES)),
        ],
        out_specs=[
            pl.BlockSpec((1, tt, GDN_DIM), lambda i, j: (i, j, 0)),
            pl.BlockSpec((1, GDN_HEADS, GDN_DK, GDN_DV), lambda i, j: (i, 0, 0, 0)),
        ],
        scratch_shapes=[
            pltpu.VMEM((tt + 8, GDN_QKV), F32),
            pltpu.VMEM((GDN_HEADS, GDN_DK, GDN_DV), F32),
        ],
        compiler_params=_params(2),
        name="gdn",
    )(qkv, z, ba, conv_hist, state, wl["conv_w"], wl["a_log"], wl["dt_bias"], wl["o_g"], wl["sel"])


def _block_kernel(x_ref, m1_ref, m2_ref, mod_ref, g_ref, hist_ref, wo1_ref, wo2_ref, wg_ref, wu_ref, cw_ref, wd_ref,
                  o_ref, hn_ref, a_scr, *, tm, nfc):
    t = pl.program_id(1)

    @pl.when(t == 0)
    def _():
        a_scr[6:8, :] = hist_ref[0]

    m = mod_ref[0]
    mix = _dot(m1_ref[0].astype(BF16), wo1_ref[...]) + _dot(m2_ref[0].astype(BF16), wo2_ref[...])
    x = x_ref[0] + m[2:3] * mix
    h = _modulate(x, g_ref[...], m[3:4], m[4:5]).astype(BF16)
    fc = D_FF // nfc
    acc = None
    for c in range(nfc):
        cs = slice(c * fc, (c + 1) * fc)
        a = _dot(h, wg_ref[:, cs])
        a_scr[8:8 + tm, cs] = a
        ac = cw_ref[2:3, cs] * a + cw_ref[1:2, cs] * a_scr[7:7 + tm, cs] + cw_ref[0:1, cs] * a_scr[6:6 + tm, cs]
        a_scr[6:8, cs] = a[tm - 2:tm, :]
        act = (_silu(ac) * _dot(h, wu_ref[:, cs])).astype(BF16)
        d = _dot(act, wd_ref[cs, :])
        acc = d if acc is None else acc + d
    o_ref[0] = x + m[5:6] * acc
    hn_ref[0] = a_scr[6:8, :]


def _block(x, m1, m2, mod_l, gain, hist, wo1, wo2, wg, wu, cw, wd):
    b, t, d = x.shape
    tm = _time_tile(t, 512)
    half = m1.shape[2]
    return pl.pallas_call(
        functools.partial(_block_kernel, tm=tm, nfc=2),
        out_shape=[
            jax.ShapeDtypeStruct((b, t, d), F32),
            jax.ShapeDtypeStruct((b, FFN_CONV - 1, D_FF), F32),
        ],
        grid=(b, t // tm),
        in_specs=[
            pl.BlockSpec((1, tm, d), lambda i, j: (i, j, 0)),
            pl.BlockSpec((1, tm, half), lambda i, j: (i, j, 0)),
            pl.BlockSpec((1, tm, half), lambda i, j: (i, j, 0)),
            pl.BlockSpec((1, 6, d), lambda i, j: (i, 0, 0)),
            _const_spec((1, d)),
            pl.BlockSpec((1, FFN_CONV - 1, D_FF), lambda i, j: (i, 0, 0)),
            _const_spec((half, d)), _const_spec((half, d)),
            _const_spec((d, D_FF)), _const_spec((d, D_FF)), _const_spec((FFN_CONV, D_FF)), _const_spec((D_FF, d)),
        ],
        out_specs=[
            pl.BlockSpec((1, tm, d), lambda i, j: (i, j, 0)),
            pl.BlockSpec((1, FFN_CONV - 1, D_FF), lambda i, j: (i, 0, 0)),
        ],
        scratch_shapes=[pltpu.VMEM((tm + 8, D_FF), F32)],
        compiler_params=_params(2),
        name="mix_out_conv_ffn",
    )(x, m1, m2, mod_l, gain, hist, wo1, wo2, wg, wu, cw, wd)


def _pad_cols(w, n):
    return jnp.pad(w, ((0, 0), (0, n - w.shape[1])))


def _lane_row(vec, offset):
    return jnp.pad(vec, (offset, LANES - offset - vec.shape[0])).reshape(1, LANES)


def _prep_even(W, i):
    f = lambda name: W[name][i]
    w_in = f("even_w_in")
    mla_cols = MLA_Q_RANK + MLA_KV_RANK
    kr_w = jnp.pad(w_in[:, mla_cols:mla_cols + MLA_ROPE], ((0, 0), (MLA_NOPE, LANES - MLA_NOPE - MLA_ROPE)))
    w_mla = jnp.concatenate([w_in[:, :mla_cols], kr_w], axis=1).astype(BF16)
    w_rw = w_in[:, mla_cols + MLA_ROPE:].astype(BF16)
    per_q = MLA_NOPE + MLA_ROPE
    w_uq = f("mla_w_uq").reshape(MLA_Q_RANK, MLA_HEADS, per_q)
    w_uq = jnp.pad(w_uq, ((0, 0), (0, 0), (0, HEAD_PAD - per_q))).reshape(MLA_Q_RANK, MLA_HEADS * HEAD_PAD)
    w_ukv = f("mla_w_ukv").reshape(MLA_KV_RANK, MLA_HEADS, MLA_NOPE + MLA_V)
    w_uk = jnp.pad(w_ukv[:, :, :MLA_NOPE], ((0, 0), (0, 0), (0, HEAD_PAD - MLA_NOPE)))
    vv = w_ukv[:, :, MLA_NOPE:]
    v_even = jnp.pad(vv, ((0, 0), (0, 0), (0, HEAD_PAD - MLA_V)))
    v_odd = jnp.pad(vv, ((0, 0), (0, 0), (HEAD_PAD - MLA_V, 0)))
    odd = (jnp.arange(MLA_HEADS) % 2 == 1)[None, :, None]
    w_uv = jnp.where(odd, v_odd, v_even)
    gi = jnp.arange(LANES)
    grp = jnp.where(gi < MLA_NOPE, 0, jnp.where(gi < per_q, 1, 2))
    gmat_q = jnp.where((grp[:, None] == grp[None, :]) & (grp[:, None] < 2),
                       jnp.where(grp[:, None] == 0, 1.0 / MLA_NOPE, 1.0 / MLA_ROPE), 0.0)
    hid = jnp.arange(256) // RWKV_HEAD
    w_out = f("even_w_out").astype(BF16)
    zpad = jnp.zeros((RWKV_HEAD, RWKV_DIM), F32)
    return {
        "w_in": [w_mla, w_rw],
        "g_qlat": f("mla_g_qlat").reshape(1, -1), "g_kvlat": f("mla_g_kvlat").reshape(1, -1),
        "w_uq": w_uq.astype(BF16),
        "g_qhead": jnp.concatenate([f("mla_g_qn"), f("mla_g_qr"), jnp.zeros((HEAD_PAD - per_q,), F32)]).reshape(1, -1),
        "g_kr": _lane_row(f("mla_g_kr"), MLA_NOPE),
        "gmat_q": gmat_q.astype(BF16),
        "w_uk": w_uk.reshape(MLA_KV_RANK, -1).astype(BF16), "w_uv": w_uv.reshape(MLA_KV_RANK, -1).astype(BF16),
        "g_kn": _lane_row(f("mla_g_kn"), 0),
        "mu": f("rwkv_mu").reshape(1, -1), "w0": f("rwkv_w0").reshape(1, -1), "a0": f("rwkv_a0").reshape(1, -1),
        "w2": jnp.concatenate([f("rwkv_w2"), zpad], axis=0).astype(BF16),
        "a2": jnp.concatenate([zpad, f("rwkv_a2")], axis=0).astype(BF16),
        "g2": f("rwkv_g2").astype(BF16),
        "k_k": f("rwkv_k_k").reshape(1, -1), "k_a": f("rwkv_k_a").reshape(1, -1),
        "r_k": f("rwkv_r_k").reshape(1, -1),
        "lnx_g": f("rwkv_lnx_g").reshape(1, -1), "lnx_b": f("rwkv_lnx_b").reshape(1, -1),
        "gsum": (hid[:, None] == hid[None, :]).astype(BF16),
        "w_out1": w_out[:MLA_HEADS * MLA_V], "w_out2": w_out[MLA_HEADS * MLA_V:],
    }


def _prep_odd(W, i):
    f = lambda name: W[name][i]
    w_in = f("odd_w_in")
    c1 = POOL_DIM + GDN_QKV
    c2 = c1 + GDN_DIM
    w_out = f("odd_w_out").astype(BF16)
    return {
        "w_in": [w_in[:, :POOL_DIM].astype(BF16), w_in[:, POOL_DIM:c1].astype(BF16), w_in[:, c1:c2].astype(BF16),
                 _pad_cols(w_in[:, c2:], LANES).astype(BF16)],
        "pool_w": f("pool_w").astype(BF16), "pool_scale": f("pool_scale").reshape(1, -1),
        "conv_w": f("gdn_conv_w"),
        "a_log": _lane_row(f("gdn_a_log"), GDN_HEADS), "dt_bias": _lane_row(f("gdn_dt_bias"), GDN_HEADS),
        "o_g": f("gdn_o_g").reshape(1, -1),
        "sel": (jnp.arange(LANES)[None, :] == (jnp.arange(8)[:, None] + GDN_HEADS)).astype(BF16),
        "w_out1": w_out[:POOL_DIM], "w_out2": w_out[POOL_DIM:],
    }


def _rope_tabs(pos0, t):
    inv = 1.0 / (ROPE_THETA ** (jnp.arange(0, MLA_ROPE, 2, dtype=F32) / MLA_ROPE))
    ang = (pos0 + jnp.arange(t, dtype=jnp.int32)).astype(F32)[:, None] * inv[None, :]
    cos, sin = jnp.cos(ang), jnp.sin(ang)
    z16 = jnp.zeros_like(cos)
    one = jnp.ones((t, MLA_NOPE), F32)
    z64 = jnp.zeros((t, MLA_NOPE), F32)
    z32 = jnp.zeros((t, LANES - MLA_NOPE - MLA_ROPE), F32)
    return (jnp.concatenate([one, cos, cos, z32], axis=1),
            jnp.concatenate([z64, -sin, z16, z32], axis=1),
            jnp.concatenate([z64, z16, sin, z32], axis=1))


def _to_blockdiag(s):
    b = s.shape[0]
    s = s.reshape(b, RWKV_HEADS // 2, 2, RWKV_HEAD, RWKV_HEAD)
    z = jnp.zeros_like(s[:, :, 0])
    top = jnp.concatenate([s[:, :, 0], z], axis=-1)
    bot = jnp.concatenate([z, s[:, :, 1]], axis=-1)
    return jnp.concatenate([top, bot], axis=-2)


def _from_blockdiag(sb):
    b = sb.shape[0]
    s0 = sb[:, :, :RWKV_HEAD, :RWKV_HEAD]
    s1 = sb[:, :, RWKV_HEAD:, RWKV_HEAD:]
    return jnp.stack([s0, s1], axis=2).reshape(b, RWKV_HEADS, RWKV_HEAD, RWKV_HEAD)


def _trunk(x, mod, pos0, caches, W, prep):
    b, t, _ = x.shape
    ckv_c, kpe_c, shift_c, wkv_c, pool_c, conv_c, gdn_c, ffn_c = caches
    tabs = _rope_tabs(pos0, t)
    o_ckv, o_kpe, o_shift, o_wkv, o_pool, o_conv, o_gdn, o_ffn = [], [], [], [], [], [], [], []
    for layer in range(DEPTH):
        i = layer // 2
        wl = prep[layer]
        mod_l = mod[layer]
        if layer % 2 == 0:
            p_mla, rw = _inproj(x, mod_l, W["norm_mix_g"][layer].reshape(1, -1), wl["w_in"])
            q, ckv_new, kpe_new, kpe_blk = _mla_prep(p_mla, tabs, wl)
            if ckv_c is None:
                ckv_all, kblk_all, past = ckv_new, kpe_blk, 0
            else:
                past = ckv_c.shape[2]
                ckv_all = jnp.concatenate([ckv_c[i], ckv_new], axis=1)
                past_blk = jnp.pad(kpe_c[i], ((0, 0), (0, 0), (MLA_NOPE, LANES - MLA_NOPE - MLA_ROPE)))
                kblk_all = jnp.concatenate([past_blk, kpe_blk], axis=1)
            kk, vv = _mla_kv(ckv_all, kblk_all, wl)
            m1 = _attention(q, kk, vv, past)
            m2, sblk = _rwkv(rw, shift_c[i][:, None, :], _to_blockdiag(wkv_c[i]), wl)
            o_ckv.append(ckv_new)
            o_kpe.append(kpe_new)
            o_shift.append(rw[:, -1])
            o_wkv.append(_from_blockdiag(sblk))
        else:
            u, qkv, z, ba = _inproj(x, mod_l, W["norm_mix_g"][layer].reshape(1, -1), wl["w_in"])
            hist16 = jnp.pad(pool_c[i], ((0, 0), (1, 0), (0, 0)))
            m1 = _pool(u, hist16, pos0, wl)
            m2, gs = _gdn(qkv, z, ba, conv_c[i], gdn_c[i], wl)
            o_pool.append(u[:, t - POOL_HIST:])
            o_conv.append(qkv[:, t - (GDN_CONV - 1):])
            o_gdn.append(gs)
        x, fc = _block(x, m1, m2, mod_l, W["norm_ffn_g"][layer].reshape(1, -1), ffn_c[layer],
                       wl["w_out1"], wl["w_out2"], wl["ffn_wg"], wl["ffn_wu"], W["ffn_conv_w"][layer], wl["ffn_wd"])
        o_ffn.append(fc)
    st = jnp.stack
    return x, (st(o_ckv), st(o_kpe), st(o_shift), st(o_wkv), st(o_pool), st(o_conv), st(o_gdn), st(o_ffn))


def _zero_caches(b, dt):
    n_even, n_odd = (DEPTH + 1) // 2, DEPTH // 2
    return (None, None,
            jnp.zeros((n_even, b, RWKV_COLS), dt),
            jnp.zeros((n_even, b, RWKV_HEADS, RWKV_HEAD, RWKV_HEAD), dt),
            jnp.zeros((n_odd, b, POOL_HIST, POOL_DIM), dt),
            jnp.zeros((n_odd, b, GDN_CONV - 1, GDN_QKV), dt),
            jnp.zeros((n_odd, b, GDN_HEADS, GDN_DK, GDN_DV), dt),
            jnp.zeros((DEPTH, b, FFN_CONV - 1, D_FF), dt))


def _run(x_prompt, x_sample, caches_s, c_prompt, c_sample, W):
    bp = x_prompt.shape[0]
    mod = _ada(jnp.concatenate([c_prompt, c_sample], axis=0), W["ada_w"], W["ada_b"])
    mod = mod.reshape(DEPTH, -1, 6, D_MODEL)
    prep = [(_prep_even if l % 2 == 0 else _prep_odd)(W, l // 2) for l in range(DEPTH)]
    for l in range(DEPTH):
        prep[l]["ffn_wg"] = W["ffn_w_gate"][l].astype(BF16)
        prep[l]["ffn_wu"] = W["ffn_w_up"][l].astype(BF16)
        prep[l]["ffn_wd"] = W["ffn_w_down"][l].astype(BF16)
    y_p, st_p = _trunk(x_prompt, mod[:, :bp], 0, _zero_caches(bp, x_prompt.dtype), W, prep)
    past_len = caches_s[0].shape[2]
    y_s, st_s = _trunk(x_sample, mod[:, bp:], past_len, caches_s, W, prep)
    return (y_p, y_s) + tuple(st_p) + tuple(st_s)


def kernel(x_prompt, x_sample, cache_mla_ckv, cache_mla_kpe, state_rwkv_shift, state_rwkv_wkv, state_pool,
           state_gdn_conv, state_gdn, state_ffn_conv, c_prompt, c_sample, ada_w, ada_b, norm_mix_g, norm_ffn_g,
           even_w_in, mla_g_qlat, mla_g_kvlat, mla_w_uq, mla_w_ukv, mla_g_qn, mla_g_qr, mla_g_kn, mla_g_kr,
           rwkv_mu, rwkv_w0, rwkv_w2, rwkv_a0, rwkv_a2, rwkv_g2, rwkv_k_k, rwkv_k_a, rwkv_r_k, rwkv_lnx_g,
           rwkv_lnx_b, even_w_out, odd_w_in, pool_w, pool_scale, gdn_conv_w, gdn_a_log, gdn_dt_bias, gdn_o_g,
           odd_w_out, ffn_w_gate, ffn_w_up, ffn_conv_w, ffn_w_down):
    W = {
        "ada_w": ada_w, "ada_b": ada_b, "norm_mix_g": norm_mix_g, "norm_ffn_g": norm_ffn_g,
        "even_w_in": even_w_in, "mla_g_qlat": mla_g_qlat, "mla_g_kvlat": mla_g_kvlat,
        "mla_w_uq": mla_w_uq, "mla_w_ukv": mla_w_ukv, "mla_g_qn": mla_g_qn, "mla_g_qr": mla_g_qr,
        "mla_g_kn": mla_g_kn, "mla_g_kr": mla_g_kr,
        "rwkv_mu": rwkv_mu, "rwkv_w0": rwkv_w0, "rwkv_w2": rwkv_w2, "rwkv_a0": rwkv_a0, "rwkv_a2": rwkv_a2,
        "rwkv_g2": rwkv_g2, "rwkv_k_k": rwkv_k_k, "rwkv_k_a": rwkv_k_a, "rwkv_r_k": rwkv_r_k,
        "rwkv_lnx_g": rwkv_lnx_g, "rwkv_lnx_b": rwkv_lnx_b, "even_w_out": even_w_out,
        "odd_w_in": odd_w_in, "pool_w": pool_w, "pool_scale": pool_scale, "gdn_conv_w": gdn_conv_w,
        "gdn_a_log": gdn_a_log, "gdn_dt_bias": gdn_dt_bias, "gdn_o_g": gdn_o_g, "odd_w_out": odd_w_out,
        "ffn_w_gate": ffn_w_gate, "ffn_w_up": ffn_w_up, "ffn_conv_w": ffn_conv_w, "ffn_w_down": ffn_w_down,
    }
    caches_s = (cache_mla_ckv, cache_mla_kpe, state_rwkv_shift, state_rwkv_wkv, state_pool, state_gdn_conv,
                state_gdn, state_ffn_conv)
    return _run(x_prompt, x_sample, caches_s, c_prompt, c_sample, W)
```

```python
import functools
import math

import jax
import jax.numpy as jnp
from jax import lax
from jax.experimental import pallas as pl
from jax.experimental.pallas import tpu as pltpu

F32 = jnp.float32
BF16 = jnp.bfloat16

D_MODEL = 1024
DEPTH = 4
CHUNK = 64
EPS = 1e-6
L2_EPS = 1e-6
NEG_INF = -1e30

MLA_HEADS = 8
MLA_NOPE = 64
MLA_ROPE = 32
MLA_V = 64
MLA_Q_RANK = 256
MLA_KV_RANK = 128
MLA_SCALE = (MLA_NOPE + MLA_ROPE) ** -0.5
ROPE_THETA = 10000.0

RWKV_HEADS = 8
RWKV_HEAD = 64
RWKV_DIM = 512
RWKV_COLS = 1792
RWKV_LN_EPS = 64e-5

POOL_DIM = 512
POOL_WINDOWS = (2, 4, 8, 16)
POOL_HIST = 15

GDN_HEADS = 4
GDN_DK = 128
GDN_DV = 128
GDN_CONV = 4
GDN_QKV = 1536
GDN_DIM = 512

D_FF = 2816
FFN_CONV = 3

LANES = 128
HEAD_PAD = 128
VMEM_LIMIT = 56 * 1024 * 1024


def _dot(a, b):
    return jnp.dot(a, b, preferred_element_type=F32)


def _dot_nt(a, b):
    return lax.dot_general(a, b, (((1,), (1,)), ((), ())), preferred_element_type=F32)


def _dot_tn(a, b):
    return lax.dot_general(a, b, (((0,), (0,)), ((), ())), preferred_element_type=F32)


def _split(x, n):
    parts = []
    r = x
    for i in range(n):
        p = r.astype(BF16)
        parts.append(p)
        if i + 1 < n:
            r = r - p.astype(F32)
    return parts


def _sum_left(mat01, x, n):
    out = None
    for p in _split(x, n):
        d = _dot(mat01, p)
        out = d if out is None else out + d
    return out


def _sum_right(x, mat01, n):
    out = None
    for p in _split(x, n):
        d = _dot(p, mat01)
        out = d if out is None else out + d
    return out


def _sigmoid(x):
    return 1.0 / (1.0 + jnp.exp(-x))


def _silu(x):
    return x * _sigmoid(x)


def _softplus(x):
    return jnp.maximum(x, 0.0) + jnp.log(1.0 + jnp.exp(-jnp.abs(x)))


def _rms(x, gain, eps=EPS):
    return x * lax.rsqrt(jnp.mean(x * x, axis=-1, keepdims=True) + eps) * gain


def _modulate(x, gain, shift, scale):
    return _rms(x, gain) * (1.0 + scale) + shift


def _solve_unit_lower(ns, rhss, steps, refine):
    size = ns[0].shape[0]
    ri = lax.broadcasted_iota(jnp.int32, (size, size), 0)
    ci = lax.broadcasted_iota(jnp.int32, (size, size), 1)
    eye = jnp.where(ri == ci, 1.0, 0.0)
    xs = [eye + n for n in ns]
    pbs = [n.astype(BF16) for n in ns]
    pbs = [_dot(p, p).astype(BF16) for p in pbs]
    for _ in range(steps - 1):
        xs = [x + _dot(x.astype(BF16), p) for x, p in zip(xs, pbs)]
        pbs = [_dot(p, p).astype(BF16) for p in pbs]
    xbs = [(x + _dot(x.astype(BF16), p)).astype(BF16) for x, p in zip(xs, pbs)]
    us = [_dot(xb, r.astype(BF16)) for xb, r in zip(xbs, rhss)]
    if not refine:
        return us
    nbs = [n.astype(BF16) for n in ns]
    usp = [_split(u, 2) for u in us]
    nus = [_dot(nb, uh) for nb, (uh, _) in zip(nbs, usp)]
    nus = [nu + _dot(nb, ul) for nu, nb, (_, ul) in zip(nus, nbs, usp)]
    return [u + _dot(xb, (r - u + nu).astype(BF16)) for u, xb, r, nu in zip(us, xbs, rhss, nus)]


def _const_spec(shape):
    nd = len(shape)
    return pl.BlockSpec(shape, lambda *_: (0,) * nd, pipeline_mode=pl.Buffered(1))


def _params(n_axes):
    return pltpu.CompilerParams(dimension_semantics=("arbitrary",) * n_axes, vmem_limit_bytes=VMEM_LIMIT)


def _time_tile(t, pref):
    return pref if t % pref == 0 else t


def _batch_block(b, t, tm, rows):
    if tm < t:
        return 1
    nb = max(1, min(b, rows // t))
    while b % nb:
        nb -= 1
    return nb


def _mod_rows(m, idx, nb, tm):
    if nb == 1:
        return m[0, idx:idx + 1, :]
    return jnp.broadcast_to(m[:, idx:idx + 1, :], (nb, tm, m.shape[-1])).reshape(nb * tm, m.shape[-1])


def _ada_kernel(c_ref, w_ref, b_ref, o_ref):
    s = _silu(c_ref[...]).astype(BF16)
    o_ref[0] = _dot(s, w_ref[0].astype(BF16)) + b_ref[0]


def _ada(c_all, ada_w, ada_b):
    n = c_all.shape[0]
    tn = 1536
    return pl.pallas_call(
        _ada_kernel,
        out_shape=jax.ShapeDtypeStruct((DEPTH, n, 6 * D_MODEL), F32),
        grid=(DEPTH, 6 * D_MODEL // tn),
        in_specs=[
            pl.BlockSpec((n, D_MODEL), lambda l, j: (0, 0)),
            pl.BlockSpec((1, D_MODEL, tn), lambda l, j: (l, 0, j)),
            pl.BlockSpec((1, 1, tn), lambda l, j: (l, 0, j)),
        ],
        out_specs=pl.BlockSpec((1, n, tn), lambda l, j: (l, 0, j)),
        compiler_params=_params(2),
        name="ada_mod",
    )(c_all, ada_w, ada_b.reshape(DEPTH, 1, 6 * D_MODEL))


def _inproj_kernel(x_ref, mod_ref, g_ref, *refs, nseg, nb, tm):
    m = mod_ref[...]
    x = x_ref[...].reshape(nb * tm, x_ref.shape[-1])
    h = _modulate(x, g_ref[...], _mod_rows(m, 0, nb, tm), _mod_rows(m, 1, nb, tm)).astype(BF16)
    for w_ref, o_ref in zip(refs[:nseg], refs[nseg:]):
        o_ref[...] = _dot(h, w_ref[...]).reshape(o_ref.shape)


def _inproj(x, mod_l, gain, w_segs):
    b, t, d = x.shape
    tm = _time_tile(t, 512)
    nb = _batch_block(b, t, tm, 512)
    nseg = len(w_segs)
    return pl.pallas_call(
        functools.partial(_inproj_kernel, nseg=nseg, nb=nb, tm=tm),
        out_shape=[jax.ShapeDtypeStruct((b, t, w.shape[1]), F32) for w in w_segs],
        grid=(b // nb, t // tm),
        in_specs=[
            pl.BlockSpec((nb, tm, d), lambda i, j: (i, j, 0)),
            pl.BlockSpec((nb, 6, d), lambda i, j: (i, 0, 0)),
            _const_spec((1, d)),
        ] + [_const_spec(w.shape) for w in w_segs],
        out_specs=[pl.BlockSpec((nb, tm, w.shape[1]), lambda i, j: (i, j, 0)) for w in w_segs],
        compiler_params=_params(2),
        name="in_proj",
    )(x, mod_l, gain, *w_segs)


def _rope(blk, c, sm, sp):
    return blk * c + pltpu.roll(blk, LANES - 16, 1) * sm + pltpu.roll(blk, 16, 1) * sp


def _mla_prep_kernel(p_ref, c_ref, sm_ref, sp_ref, gq_ref, gkv_ref, wuq_ref, ghead_ref, gkr_ref, gmat_ref,
                     q_ref, ckv_ref, kpe_ref, kblk_ref):
    p = p_ref[0]
    c, sm, sp = c_ref[...], sm_ref[...], sp_ref[...]
    qn = _rms(p[:, :MLA_Q_RANK], gq_ref[...]).astype(BF16)
    q = _dot(qn, wuq_ref[...])
    gmat = gmat_ref[...]
    for h in range(MLA_HEADS):
        hb = q[:, h * HEAD_PAD:(h + 1) * HEAD_PAD]
        ms = _sum_right(hb * hb, gmat, 2)
        hb = hb * lax.rsqrt(ms + EPS) * ghead_ref[...]
        q_ref[0, h] = _rope(hb, c, sm, sp).astype(BF16)
    ckv_ref[0] = _rms(p[:, MLA_Q_RANK:MLA_Q_RANK + MLA_KV_RANK], gkv_ref[...])
    kr = p[:, MLA_Q_RANK + MLA_KV_RANK:]
    ms = jnp.sum(kr * kr, axis=-1, keepdims=True) * (1.0 / MLA_ROPE)
    kb = _rope(kr * lax.rsqrt(ms + EPS) * gkr_ref[...], c, sm, sp)
    kblk_ref[0] = kb
    kpe_ref[0] = kb[:, MLA_NOPE:MLA_NOPE + MLA_ROPE]


def _mla_prep(p_mla, tabs, wl):
    b, t, _ = p_mla.shape
    tm = _time_tile(t, 512)
    tab_spec = pl.BlockSpec((tm, LANES), lambda i, j: (j, 0))
    return pl.pallas_call(
        _mla_prep_kernel,
        out_shape=[
            jax.ShapeDtypeStruct((b, MLA_HEADS, t, HEAD_PAD), BF16),
            jax.ShapeDtypeStruct((b, t, MLA_KV_RANK), F32),
            jax.ShapeDtypeStruct((b, t, MLA_ROPE), F32),
            jax.ShapeDtypeStruct((b, t, LANES), F32),
        ],
        grid=(b, t // tm),
        in_specs=[
            pl.BlockSpec((1, tm, 512), lambda i, j: (i, j, 0)),
            tab_spec, tab_spec, tab_spec,
            _const_spec((1, MLA_Q_RANK)), _const_spec((1, MLA_KV_RANK)),
            _const_spec((MLA_Q_RANK, MLA_HEADS * HEAD_PAD)),
            _const_spec((1, LANES)), _const_spec((1, LANES)), _const_spec((LANES, LANES)),
        ],
        out_specs=[
            pl.BlockSpec((1, MLA_HEADS, tm, HEAD_PAD), lambda i, j: (i, 0, j, 0)),
            pl.BlockSpec((1, tm, MLA_KV_RANK), lambda i, j: (i, j, 0)),
            pl.BlockSpec((1, tm, MLA_ROPE), lambda i, j: (i, j, 0)),
            pl.BlockSpec((1, tm, LANES), lambda i, j: (i, j, 0)),
        ],
        compiler_params=_params(2),
        name="mla_prep",
    )(p_mla, tabs[0], tabs[1], tabs[2], wl["g_qlat"], wl["g_kvlat"], wl["w_uq"], wl["g_qhead"], wl["g_kr"],
      wl["gmat_q"])


def _ones_lane(h):
    return MLA_V if h % 2 == 0 else 0


def _mla_kv_kernel(ckv_ref, kpe_ref, wuk_ref, wuv_ref, gkn_ref, k_ref, v_ref):
    cb = ckv_ref[0].astype(BF16)
    kf = _dot(cb, wuk_ref[...])
    vf = _dot(cb, wuv_ref[...])
    kpe = kpe_ref[0]
    lane = lax.broadcasted_iota(jnp.int32, (1, LANES), 1)
    for h in range(MLA_HEADS):
        kb = kf[:, h * HEAD_PAD:(h + 1) * HEAD_PAD]
        ms = jnp.sum(kb * kb, axis=-1, keepdims=True) * (1.0 / MLA_NOPE)
        k_ref[0, h] = (kb * lax.rsqrt(ms + EPS) * gkn_ref[...] + kpe).astype(BF16)
        ones = jnp.where(lane == _ones_lane(h), 1.0, 0.0)
        v_ref[0, h] = (vf[:, h * HEAD_PAD:(h + 1) * HEAD_PAD] + ones).astype(BF16)


def _mla_kv(ckv_all, kpe_blk, wl):
    b, l, _ = ckv_all.shape
    tl = _time_tile(l, 512)
    return pl.pallas_call(
        _mla_kv_kernel,
        out_shape=[jax.ShapeDtypeStruct((b, MLA_HEADS, l, HEAD_PAD), BF16)] * 2,
        grid=(b, l // tl),
        in_specs=[
            pl.BlockSpec((1, tl, MLA_KV_RANK), lambda i, j: (i, j, 0)),
            pl.BlockSpec((1, tl, LANES), lambda i, j: (i, j, 0)),
            _const_spec((MLA_KV_RANK, MLA_HEADS * HEAD_PAD)),
            _const_spec((MLA_KV_RANK, MLA_HEADS * HEAD_PAD)),
            _const_spec((1, LANES)),
        ],
        out_specs=[pl.BlockSpec((1, MLA_HEADS, tl, HEAD_PAD), lambda i, j: (i, 0, j, 0))] * 2,
        compiler_params=_params(2),
        name="mla_kv",
    )(ckv_all, kpe_blk, wl["w_uk"], wl["w_uv"], wl["g_kn"])


def _attn_kernel(q_ref, k_ref, v_ref, o_ref, m_scr, acc_scr, *, tq, tk, past_len):
    d0 = pl.multiple_of(past_len + pl.program_id(1) * tq, tq)
    nprefix = d0 // tk
    qc = lax.broadcasted_iota(jnp.int32, (tq, tq), 0) // CHUNK
    kc = lax.broadcasted_iota(jnp.int32, (tq, tq), 1) // CHUNK
    mask = kc <= qc
    heads = range(MLA_HEADS)

    def step(r0, n, first):
        ss = [_dot_nt(q_ref[0, h], k_ref[0, h, pl.ds(r0, n), :]) for h in heads]
        for h in heads:
            s = ss[h] * MLA_SCALE
            if first:
                s = jnp.where(mask, s, NEG_INF)
            mx = jnp.max(s, axis=-1, keepdims=True)
            m_old = m_scr[h]
            mn = jnp.broadcast_to(mx, (tq, LANES)) if first else jnp.maximum(m_old, mx)
            m_scr[h] = mn
            mw = jnp.concatenate([mn] * (n // LANES), axis=1) if n >= LANES else mn[:, :n]
            pv = _dot(jnp.exp(s - mw).astype(BF16), v_ref[0, h, pl.ds(r0, n), :])
            acc_scr[h] = pv if first else jnp.exp(m_old - mn) * acc_scr[h] + pv

    step(d0, tq, True)

    def body(j, carry):
        step(pl.multiple_of(j * tk, tk), tk, False)
        return carry

    lax.fori_loop(0, nprefix, body, 0)
    lane = lax.broadcasted_iota(jnp.int32, (1, LANES), 1)
    outs = []
    for h in heads:
        acc = acc_scr[h]
        c = _ones_lane(h)
        own = (lane < MLA_V) if h % 2 == 0 else (lane >= MLA_V)
        outs.append(jnp.where(own, acc / acc[:, c:c + 1], 0.0))
    for j in range(MLA_HEADS // 2):
        o_ref[0, :, j * LANES:(j + 1) * LANES] = outs[2 * j] + outs[2 * j + 1]


def _attention(q, k, v, past_len):
    b, hh, t, _ = q.shape
    l = k.shape[2]
    tq = _time_tile(t, 256)
    tk = 256
    assert past_len % tk == 0 and (tq % tk == 0 or past_len + tq == l) and l == past_len + t
    return pl.pallas_call(
        functools.partial(_attn_kernel, tq=tq, tk=tk, past_len=past_len),
        out_shape=jax.ShapeDtypeStruct((b, t, MLA_HEADS * MLA_V), F32),
        grid=(b, t // tq),
        in_specs=[
            pl.BlockSpec((1, hh, tq, HEAD_PAD), lambda i, j: (i, 0, j, 0)),
            pl.BlockSpec((1, hh, l, HEAD_PAD), lambda i, j: (i, 0, 0, 0)),
            pl.BlockSpec((1, hh, l, HEAD_PAD), lambda i, j: (i, 0, 0, 0)),
        ],
        out_specs=pl.BlockSpec((1, tq, MLA_HEADS * MLA_V), lambda i, j: (i, j, 0)),
        scratch_shapes=[pltpu.VMEM((MLA_HEADS, tq, LANES), F32), pltpu.VMEM((MLA_HEADS, tq, LANES), F32)],
        compiler_params=_params(2),
        name="mla_attn",
    )(q, k, v)


def _rwkv_kernel(rw_ref, sh_ref, s0_ref, mu_ref, w0_ref, w2_ref, a0_ref, a2_ref, g2_ref, kkw_ref, ka_ref,
                 rk_ref, lng_ref, lnb_ref, gsum_ref, y_ref, s_ref, xs_scr, st_scr, *, tt, cc):
    t = pl.program_id(1)

    @pl.when(t == 0)
    def _():
        xs_scr[7:8, :] = sh_ref[0]
        st_scr[...] = s0_ref[0]

    rw = rw_ref[0]
    xs_scr[8:8 + tt, :] = rw
    prev = xs_scr[7:7 + tt, :]
    xm = rw + (prev - rw) * mu_ref[...]
    xs_scr[7:8, :] = rw[tt - 1:tt, :]

    r = xm[:, 0:RWKV_DIM]
    k = xm[:, RWKV_DIM:2 * RWKV_DIM]
    v = xm[:, 2 * RWKV_DIM:3 * RWKV_DIM]
    lw = xm[:, 3 * RWKV_DIM:3 * RWKV_DIM + LANES]
    dg = xm[:, 3 * RWKV_DIM + LANES:]
    zw = w0_ref[...] + _dot(jnp.tanh(lw).astype(BF16), w2_ref[...])
    ld = -jnp.exp(-_softplus(-zw) - 0.5)
    a = _sigmoid(a0_ref[...] + _dot(lw.astype(BF16), a2_ref[...]))
    g = _dot(_sigmoid(dg).astype(BF16), g2_ref[...])

    gmat = gsum_ref[...]

    def gsum(x):
        return jnp.concatenate([_sum_right(x[:, :256], gmat, 2), _sum_right(x[:, 256:], gmat, 2)], axis=-1)

    kkr = k * kkw_ref[...]
    kk = kkr * lax.rsqrt(gsum(kkr * kkr) + L2_EPS)
    k2 = k * (1.0 + (a - 1.0) * ka_ref[...])
    kka = kk * a
    bonus = gsum(r * k2 * rk_ref[...]) * v

    c2 = 2 * cc
    nch = tt // cc
    npair = RWKV_HEADS // 2
    ri = lax.broadcasted_iota(jnp.int32, (tt, tt), 0)
    ci = lax.broadcasted_iota(jnp.int32, (tt, tt), 1)
    tri = jnp.where(ci <= ri, jnp.where(ci // cc == ri // cc, 1.0, 0.0), 0.0).astype(BF16)
    rl = lax.broadcasted_iota(jnp.int32, (c2, 2 * c2), 0) % cc
    cl = lax.broadcasted_iota(jnp.int32, (c2, 2 * c2), 1) % cc
    strict = cl < rl
    incl = cl <= rl
    low = lax.broadcasted_iota(jnp.int32, (1, LANES), 1) < RWKV_HEAD

    cs = _sum_left(tri, ld, 3)
    tots = [cs[(c + 1) * cc - 1:(c + 1) * cc, :] for c in range(nch)]
    tot = jnp.concatenate([jnp.broadcast_to(tc, (cc, RWKV_DIM)) for tc in tots], axis=0) if nch > 1 else tots[0]
    pinv = jnp.exp(-cs)
    pend = jnp.exp(tot - cs)
    at = -(kk * jnp.exp(cs - ld))
    rt = r * jnp.exp(cs)
    bt = kka * pinv
    kt = k2 * pinv
    bh = kka * pend
    kh = k2 * pend

    inst = [(c, j) for c in range(nch) for j in range(npair)]

    def blk(x, c, j):
        xp = x[c * cc:(c + 1) * cc, j * LANES:(j + 1) * LANES]
        return jnp.concatenate([jnp.where(low, xp, 0.0), jnp.where(low, 0.0, xp)], axis=0)

    a_b = [blk(at, c, j) for c, j in inst]
    r_b = [blk(rt, c, j) for c, j in inst]
    v_b = [blk(v, c, j) for c, j in inst]
    bk_b = [jnp.concatenate([blk(bh, c, j), blk(kh, c, j)], axis=0).astype(BF16) for c, j in inst]
    lhs = [jnp.concatenate([a, rr], axis=0).astype(BF16) for a, rr in zip(a_b, r_b)]
    rhs = [jnp.concatenate([blk(bt, c, j), blk(kt, c, j)], axis=0).astype(BF16) for c, j in inst]
    mm = [_dot_nt(l, rr) for l, rr in zip(lhs, rhs)]
    mt = [jnp.where(strict, m[:c2], 0.0) for m in mm]
    mb = [jnp.where(incl, m[c2:], 0.0).astype(BF16) for m in mm]
    akv = [_dot(m.astype(BF16), jnp.concatenate([jnp.zeros_like(vb), vb], axis=0).astype(BF16))
           for m, vb in zip(mt, v_b)]
    zs = _solve_unit_lower([m[:, :c2] for m in mt], [jnp.concatenate([a, w], axis=1) for a, w in zip(a_b, akv)],
                           int(math.log2(cc)) - 1, refine=False)
    ahat = [z[:, :LANES].astype(BF16) for z in zs]
    wv = [jnp.concatenate([z[:, LANES:], vb], axis=0).astype(BF16) for z, vb in zip(zs, v_b)]
    rhat = [(rr + _dot(m[:, :c2], ah)).astype(BF16) for rr, m, ah in zip(r_b, mb, ahat)]
    yc = [_dot(m, w) for m, w in zip(mb, wv)]
    phi = [_dot_tn(ah, bk[:c2]).astype(BF16) for ah, bk in zip(ahat, bk_b)]
    gam = [_dot_tn(w, bk) for w, bk in zip(wv, bk_b)]

    sb = [st_scr[j] for j in range(npair)]
    ys = []
    for c in range(nch):
        ptot = jnp.exp(tots[c])
        sbb = [s.astype(BF16) for s in sb]
        yb = [_dot_nt(rhat[c * npair + j], sbb[j]) + yc[c * npair + j] for j in range(npair)]
        sb = [sb[j] * ptot[:, j * LANES:(j + 1) * LANES] + _dot(sbb[j], phi[c * npair + j]) + gam[c * npair + j]
              for j in range(npair)]
        ys.append(jnp.concatenate([yy[:cc] + yy[cc:] for yy in yb], axis=-1))
    for j in range(npair):
        st_scr[j] = sb[j]
    y = jnp.concatenate(ys, axis=0) if nch > 1 else ys[0]

    mu = gsum(y) * (1.0 / RWKV_HEAD)
    dy = y - mu
    var = gsum(dy * dy) * (1.0 / RWKV_HEAD)
    yn = dy * lax.rsqrt(var + RWKV_LN_EPS) * lng_ref[...] + lnb_ref[...]
    y_ref[0] = (yn + bonus) * g
    s_ref[0] = st_scr[...]


def _rwkv(rw, shift, sblk, wl):
    b, t, _ = rw.shape
    cc = min(t, CHUNK)
    tt = _time_tile(t, 256)
    row = lambda n: _const_spec((1, n))
    return pl.pallas_call(
        functools.partial(_rwkv_kernel, tt=tt, cc=cc),
        out_shape=[
            jax.ShapeDtypeStruct((b, t, RWKV_DIM), F32),
            jax.ShapeDtypeStruct((b, RWKV_HEADS // 2, LANES, LANES), F32),
        ],
        grid=(b, t // tt),
        in_specs=[
            pl.BlockSpec((1, tt, RWKV_COLS), lambda i, j: (i, j, 0)),
            pl.BlockSpec((1, 1, RWKV_COLS), lambda i, j: (i, 0, 0)),
            pl.BlockSpec((1, RWKV_HEADS // 2, LANES, LANES), lambda i, j: (i, 0, 0, 0)),
            row(RWKV_COLS), row(RWKV_DIM), _const_spec((LANES, RWKV_DIM)), row(RWKV_DIM),
            _const_spec((LANES, RWKV_DIM)), _const_spec((LANES, RWKV_DIM)),
            row(RWKV_DIM), row(RWKV_DIM), row(RWKV_DIM), row(RWKV_DIM), row(RWKV_DIM),
            _const_spec((256, 256)),
        ],
        out_specs=[
            pl.BlockSpec((1, tt, RWKV_DIM), lambda i, j: (i, j, 0)),
            pl.BlockSpec((1, RWKV_HEADS // 2, LANES, LANES), lambda i, j: (i, 0, 0, 0)),
        ],
        scratch_shapes=[
            pltpu.VMEM((tt + 8, RWKV_COLS), F32),
            pltpu.VMEM((RWKV_HEADS // 2, LANES, LANES), F32),
        ],
        compiler_params=_params(2),
        name="rwkv7",
    )(rw, shift, sblk, wl["mu"], wl["w0"], wl["w2"], wl["a0"], wl["a2"], wl["g2"], wl["k_k"], wl["k_a"],
      wl["r_k"], wl["lnx_g"], wl["lnx_b"], wl["gsum"])


def _pool_kernel(u_ref, hist_ref, pw_ref, ps_ref, o_ref, scr, *, tt, pos0):
    t = pl.program_id(1)

    @pl.when(t == 0)
    def _():
        scr[0:16, :] = hist_ref[0]

    u = u_ref[0]
    scr[16:16 + tt, :] = u
    pos = pos0 + t * tt + lax.broadcasted_iota(jnp.int32, (tt, 1), 0)
    for gi, w in enumerate(POOL_WINDOWS):
        ls = slice(gi * LANES, (gi + 1) * LANES)
        ug = u[:, ls]
        s = ug
        for d in range(1, w):
            s = s + scr[16 - d:16 - d + tt, ls]
        cnt = jnp.minimum(w, pos + 1).astype(F32)
        diff = s / cnt - ug
        o_ref[0, :, ls] = _dot(diff.astype(BF16), pw_ref[gi]) * ps_ref[:, ls]
    scr[0:16, :] = scr[tt:tt + 16, :]


def _pool(u, hist16, pos0, wl):
    b, t, _ = u.shape
    tt = _time_tile(t, 256)
    return pl.pallas_call(
        functools.partial(_pool_kernel, tt=tt, pos0=pos0),
        out_shape=jax.ShapeDtypeStruct((b, t, POOL_DIM), F32),
        grid=(b, t // tt),
        in_specs=[
            pl.BlockSpec((1, tt, POOL_DIM), lambda i, j: (i, j, 0)),
            pl.BlockSpec((1, 16, POOL_DIM), lambda i, j: (i, 0, 0)),
            _const_spec((len(POOL_WINDOWS), LANES, LANES)),
            _const_spec((1, POOL_DIM)),
        ],
        out_specs=pl.BlockSpec((1, tt, POOL_DIM), lambda i, j: (i, j, 0)),
        scratch_shapes=[pltpu.VMEM((tt + 16, POOL_DIM), F32)],
        compiler_params=_params(2),
        name="pool_mix",
    )(u, hist16, wl["pool_w"], wl["pool_scale"])


def _gdn_kernel(qkv_ref, z_ref, ba_ref, hist_ref, s0_ref, cw_ref, alog_ref, dtb_ref, og_ref, sel_ref,
                o_ref, s_ref, x_scr, st_scr, *, tt, cc):
    t = pl.program_id(1)

    @pl.when(t == 0)
    def _():
        x_scr[5:8, :] = hist_ref[0]
        st_scr[...] = s0_ref[0]

    x = qkv_ref[0]
    x_scr[8:8 + tt, :] = x
    cw = cw_ref[...]
    xc = cw[3:4] * x + cw[2:3] * x_scr[7:7 + tt, :] + cw[1:2] * x_scr[6:6 + tt, :] + cw[0:1] * x_scr[5:5 + tt, :]
    x_scr[5:8, :] = x[tt - 3:tt, :]
    xc = _silu(xc)
    ba = ba_ref[0]
    beta_all = _sigmoid(ba)
    g_all = -jnp.exp(alog_ref[...]) * _softplus(ba + dtb_ref[...])

    nch = tt // cc
    ri = lax.broadcasted_iota(jnp.int32, (cc, cc), 0)
    ci = lax.broadcasted_iota(jnp.int32, (cc, cc), 1)
    incl = ci <= ri
    strict = ci < ri
    rt_ = lax.broadcasted_iota(jnp.int32, (tt, tt), 0)
    ct_ = lax.broadcasted_iota(jnp.int32, (tt, tt), 1)
    tri = jnp.where(ct_ <= rt_, jnp.where(ct_ // cc == rt_ // cc, 1.0, 0.0), 0.0).astype(BF16)
    sel = sel_ref[...]
    zz = z_ref[0]

    gcs = _sum_left(tri, g_all, 3)
    grows = []
    for c in range(nch):
        parts = _split(gcs[c * cc:(c + 1) * cc], 3)
        grows.append(_dot_nt(sel, parts[0]) + _dot_nt(sel, parts[1]) + _dot_nt(sel, parts[2]))

    qn, kn = [], []
    for h in range(GDN_HEADS):
        qh = xc[:, h * GDN_DK:(h + 1) * GDN_DK]
        kh = xc[:, (GDN_HEADS + h) * GDN_DK:(GDN_HEADS + h + 1) * GDN_DK]
        qn.append(qh * lax.rsqrt(jnp.sum(qh * qh, axis=-1, keepdims=True) + L2_EPS) * (GDN_DK ** -0.5))
        kn.append(kh * lax.rsqrt(jnp.sum(kh * kh, axis=-1, keepdims=True) + L2_EPS))

    inst = [(c, h) for c in range(nch) for h in range(GDN_HEADS)]
    v0 = 2 * GDN_HEADS * GDN_DK

    qs = [qn[h][c * cc:(c + 1) * cc] for c, h in inst]
    ks = [kn[h][c * cc:(c + 1) * cc] for c, h in inst]
    vs = [xc[c * cc:(c + 1) * cc, v0 + h * GDN_DV:v0 + (h + 1) * GDN_DV] for c, h in inst]
    betas = [beta_all[c * cc:(c + 1) * cc, h:h + 1] for c, h in inst]
    gcols = [gcs[c * cc:(c + 1) * cc, GDN_HEADS + h:GDN_HEADS + h + 1] for c, h in inst]
    glasts = [g[cc - 1:cc, :] for g in gcols]
    egs = [jnp.exp(g) for g in gcols]
    decs = [jnp.where(incl, jnp.exp(jnp.where(incl, g - grows[c][h:h + 1, :], 0.0)), 0.0)
            for g, (c, h) in zip(gcols, inst)]
    grams = [_dot_nt(jnp.concatenate([k_, q_], axis=0).astype(BF16), k_.astype(BF16)) for k_, q_ in zip(ks, qs)]
    n_mats = [-(b_ * gm[:cc] * jnp.where(strict, d_, 0.0)) for b_, gm, d_ in zip(betas, grams, decs)]
    qks = [(gm[cc:] * d_).astype(BF16) for gm, d_ in zip(grams, decs)]
    zs = _solve_unit_lower(n_mats, [jnp.concatenate([b_ * e_ * k_, b_ * v_], axis=1)
                                    for b_, e_, k_, v_ in zip(betas, egs, ks, vs)], int(math.log2(cc)) - 1,
                           refine=True)
    zb = [z.astype(BF16) for z in zs]
    qkz = [_dot(qk, z) for qk, z in zip(qks, zb)]
    qhat = [(e_ * q_ - t_[:, :GDN_DK]).astype(BF16) for e_, q_, t_ in zip(egs, qs, qkz)]
    pg = [_dot_tn((k_ * jnp.exp(gl - g)).astype(BF16), z) for k_, gl, g, z in zip(ks, glasts, gcols, zb)]

    sts = [st_scr[h] for h in range(GDN_HEADS)]
    os_ = []
    for c in range(nch):
        i0 = c * GDN_HEADS
        stb = [s_.astype(BF16) for s_ in sts]
        oo = [_dot(qhat[i0 + h], stb[h]) + qkz[i0 + h][:, GDN_DK:] for h in range(GDN_HEADS)]
        sts = [jnp.exp(glasts[i0 + h]) * sts[h] - _dot(pg[i0 + h][:, :GDN_DK].astype(BF16), stb[h])
               + pg[i0 + h][:, GDN_DK:] for h in range(GDN_HEADS)]
        os_.append(jnp.concatenate(
            [_rms(oo[h], og_ref[...]) * _silu(zz[c * cc:(c + 1) * cc, h * GDN_DV:(h + 1) * GDN_DV])
             for h in range(GDN_HEADS)], axis=-1))
    for h in range(GDN_HEADS):
        st_scr[h] = sts[h]
    o_ref[0] = jnp.concatenate(os_, axis=0) if nch > 1 else os_[0]
    s_ref[0] = st_scr[...]


def _gdn(qkv, z, ba, conv_hist, state, wl):
    b, t, _ = qkv.shape
    cc = min(t, CHUNK)
    tt = _time_tile(t, 256)
    return pl.pallas_call(
        functools.partial(_gdn_kernel, tt=tt, cc=cc),
        out_shape=[
            jax.ShapeDtypeStruct((b, t, GDN_DIM), F32),
            jax.ShapeDtypeStruct((b, GDN_HEADS, GDN_DK, GDN_DV), F32),
        ],
        grid=(b, t // tt),
        in_specs=[
            pl.BlockSpec((1, tt, GDN_QKV), lambda i, j: (i, j, 0)),
            pl.BlockSpec((1, tt, GDN_DIM), lambda i, j: (i, j, 0)),
            pl.BlockSpec((1, tt, LANES), lambda i, j: (i, j, 0)),
            pl.BlockSpec((1, GDN_CONV - 1, GDN_QKV), lambda i, j: (i, 0, 0)),
            pl.BlockSpec((1, GDN_HEADS, GDN_DK, GDN_DV), lambda i, j: (i, 0, 0, 0)),
            _const_spec((GDN_CONV, GDN_QKV)),
            _const_spec((1, LANES)), _const_spec((1, LANES)), _const_spec((1, GDN_DV)),
            _const_spec((8, LANES)),
        ],
        out_specs=[
            pl.BlockSpec((1, tt, GDN_DIM), lambda i, j: (i, j, 0)),
            pl.BlockSpec((1, GDN_HEADS, GDN_DK, GDN_DV), lambda i, j: (i, 0, 0, 0)),
        ],
        scratch_shapes=[
            pltpu.VMEM((tt + 8, GDN_QKV), F32),
            pltpu.VMEM((GDN_HEADS, GDN_DK, GDN_DV), F32),
        ],
        compiler_params=_params(2),
        name="gdn",
    )(qkv, z, ba, conv_hist, state, wl["conv_w"], wl["a_log"], wl["dt_bias"], wl["o_g"], wl["sel"])


def _block_kernel(x_ref, m1_ref, m2_ref, mod_ref, g_ref, hist_ref, wo1_ref, wo2_ref, wg_ref, wu_ref, cw_ref, wd_ref,
                  o_ref, hn_ref, a_scr, *, tm, nb, nfc):
    t = pl.program_id(1)
    rows = nb * tm

    @pl.when(t == 0)
    def _():
        a_scr[:, 6:8, :] = hist_ref[...]

    m = mod_ref[...]
    mix = (_dot(m1_ref[...].reshape(rows, -1).astype(BF16), wo1_ref[...])
           + _dot(m2_ref[...].reshape(rows, -1).astype(BF16), wo2_ref[...]))
    x = x_ref[...].reshape(rows, -1) + _mod_rows(m, 2, nb, tm) * mix
    h = _modulate(x, g_ref[...], _mod_rows(m, 3, nb, tm), _mod_rows(m, 4, nb, tm)).astype(BF16)
    fc = D_FF // nfc
    acc = None
    for c in range(nfc):
        cs = slice(c * fc, (c + 1) * fc)
        a = _dot(h, wg_ref[:, cs])
        a3 = a.reshape(nb, tm, fc)
        a_scr[:, 8:8 + tm, cs] = a3
        ac = (cw_ref[2:3, cs] * a + cw_ref[1:2, cs] * a_scr[:, 7:7 + tm, cs].reshape(rows, fc)
              + cw_ref[0:1, cs] * a_scr[:, 6:6 + tm, cs].reshape(rows, fc))
        a_scr[:, 6:8, cs] = a3[:, tm - 2:tm, :]
        act = (_silu(ac) * _dot(h, wu_ref[:, cs])).astype(BF16)
        d = _dot(act, wd_ref[cs, :])
        acc = d if acc is None else acc + d
    o_ref[...] = (x + _mod_rows(m, 5, nb, tm) * acc).reshape(o_ref.shape)
    hn_ref[...] = a_scr[:, 6:8, :]


def _block(x, m1, m2, mod_l, gain, hist, wo1, wo2, wg, wu, cw, wd):
    b, t, d = x.shape
    tm = _time_tile(t, 512)
    nb = _batch_block(b, t, tm, 512)
    half = m1.shape[2]
    return pl.pallas_call(
        functools.partial(_block_kernel, tm=tm, nb=nb, nfc=2),
        out_shape=[
            jax.ShapeDtypeStruct((b, t, d), F32),
            jax.ShapeDtypeStruct((b, FFN_CONV - 1, D_FF), F32),
        ],
        grid=(b // nb, t // tm),
        in_specs=[
            pl.BlockSpec((nb, tm, d), lambda i, j: (i, j, 0)),
            pl.BlockSpec((nb, tm, half), lambda i, j: (i, j, 0)),
            pl.BlockSpec((nb, tm, half), lambda i, j: (i, j, 0)),
            pl.BlockSpec((nb, 6, d), lambda i, j: (i, 0, 0)),
            _const_spec((1, d)),
            pl.BlockSpec((nb, FFN_CONV - 1, D_FF), lambda i, j: (i, 0, 0)),
            _const_spec((half, d)), _const_spec((half, d)),
            _const_spec((d, D_FF)), _const_spec((d, D_FF)), _const_spec((FFN_CONV, D_FF)), _const_spec((D_FF, d)),
        ],
        out_specs=[
            pl.BlockSpec((nb, tm, d), lambda i, j: (i, j, 0)),
            pl.BlockSpec((nb, FFN_CONV - 1, D_FF), lambda i, j: (i, 0, 0)),
        ],
        scratch_shapes=[pltpu.VMEM((nb, tm + 8, D_FF), F32)],
        compiler_params=_params(2),
        name="mix_out_conv_ffn",
    )(x, m1, m2, mod_l, gain, hist, wo1, wo2, wg, wu, cw, wd)


def _pad_cols(w, n):
    return jnp.pad(w, ((0, 0), (0, n - w.shape[1])))


def _lane_row(vec, offset):
    return jnp.pad(vec, (offset, LANES - offset - vec.shape[0])).reshape(1, LANES)


def _prep_even(W, i):
    f = lambda name: W[name][i]
    w_in = f("even_w_in")
    mla_cols = MLA_Q_RANK + MLA_KV_RANK
    kr_w = jnp.pad(w_in[:, mla_cols:mla_cols + MLA_ROPE], ((0, 0), (MLA_NOPE, LANES - MLA_NOPE - MLA_ROPE)))
    w_mla = jnp.concatenate([w_in[:, :mla_cols], kr_w], axis=1).astype(BF16)
    w_rw = w_in[:, mla_cols + MLA_ROPE:].astype(BF16)
    per_q = MLA_NOPE + MLA_ROPE
    w_uq = f("mla_w_uq").reshape(MLA_Q_RANK, MLA_HEADS, per_q)
    w_uq = jnp.pad(w_uq, ((0, 0), (0, 0), (0, HEAD_PAD - per_q))).reshape(MLA_Q_RANK, MLA_HEADS * HEAD_PAD)
    w_ukv = f("mla_w_ukv").reshape(MLA_KV_RANK, MLA_HEADS, MLA_NOPE + MLA_V)
    w_uk = jnp.pad(w_ukv[:, :, :MLA_NOPE], ((0, 0), (0, 0), (0, HEAD_PAD - MLA_NOPE)))
    vv = w_ukv[:, :, MLA_NOPE:]
    v_even = jnp.pad(vv, ((0, 0), (0, 0), (0, HEAD_PAD - MLA_V)))
    v_odd = jnp.pad(vv, ((0, 0), (0, 0), (HEAD_PAD - MLA_V, 0)))
    odd = (jnp.arange(MLA_HEADS) % 2 == 1)[None, :, None]
    w_uv = jnp.where(odd, v_odd, v_even)
    gi = jnp.arange(LANES)
    grp = jnp.where(gi < MLA_NOPE, 0, jnp.where(gi < per_q, 1, 2))
    gmat_q = jnp.where((grp[:, None] == grp[None, :]) & (grp[:, None] < 2),
                       jnp.where(grp[:, None] == 0, 1.0 / MLA_NOPE, 1.0 / MLA_ROPE), 0.0)
    hid = jnp.arange(256) // RWKV_HEAD
    w_out = f("even_w_out").astype(BF16)
    zpad = jnp.zeros((RWKV_HEAD, RWKV_DIM), F32)
    return {
        "w_in": [w_mla, w_rw],
        "g_qlat": f("mla_g_qlat").reshape(1, -1), "g_kvlat": f("mla_g_kvlat").reshape(1, -1),
        "w_uq": w_uq.astype(BF16),
        "g_qhead": jnp.concatenate([f("mla_g_qn"), f("mla_g_qr"), jnp.zeros((HEAD_PAD - per_q,), F32)]).reshape(1, -1),
        "g_kr": _lane_row(f("mla_g_kr"), MLA_NOPE),
        "gmat_q": gmat_q.astype(BF16),
        "w_uk": w_uk.reshape(MLA_KV_RANK, -1).astype(BF16), "w_uv": w_uv.reshape(MLA_KV_RANK, -1).astype(BF16),
        "g_kn": _lane_row(f("mla_g_kn"), 0),
        "mu": f("rwkv_mu").reshape(1, -1), "w0": f("rwkv_w0").reshape(1, -1), "a0": f("rwkv_a0").reshape(1, -1),
        "w2": jnp.concatenate([f("rwkv_w2"), zpad], axis=0).astype(BF16),
        "a2": jnp.concatenate([zpad, f("rwkv_a2")], axis=0).astype(BF16),
        "g2": f("rwkv_g2").astype(BF16),
        "k_k": f("rwkv_k_k").reshape(1, -1), "k_a": f("rwkv_k_a").reshape(1, -1),
        "r_k": f("rwkv_r_k").reshape(1, -1),
        "lnx_g": f("rwkv_lnx_g").reshape(1, -1), "lnx_b": f("rwkv_lnx_b").reshape(1, -1),
        "gsum": (hid[:, None] == hid[None, :]).astype(BF16),
        "w_out1": w_out[:MLA_HEADS * MLA_V], "w_out2": w_out[MLA_HEADS * MLA_V:],
    }


def _prep_odd(W, i):
    f = lambda name: W[name][i]
    w_in = f("odd_w_in")
    c1 = POOL_DIM + GDN_QKV
    c2 = c1 + GDN_DIM
    w_out = f("odd_w_out").astype(BF16)
    return {
        "w_in": [w_in[:, :POOL_DIM].astype(BF16), w_in[:, POOL_DIM:c1].astype(BF16), w_in[:, c1:c2].astype(BF16),
                 _pad_cols(w_in[:, c2:], LANES).astype(BF16)],
        "pool_w": f("pool_w").astype(BF16), "pool_scale": f("pool_scale").reshape(1, -1),
        "conv_w": f("gdn_conv_w"),
        "a_log": _lane_row(f("gdn_a_log"), GDN_HEADS), "dt_bias": _lane_row(f("gdn_dt_bias"), GDN_HEADS),
        "o_g": f("gdn_o_g").reshape(1, -1),
        "sel": (jnp.arange(LANES)[None, :] == (jnp.arange(8)[:, None] + GDN_HEADS)).astype(BF16),
        "w_out1": w_out[:POOL_DIM], "w_out2": w_out[POOL_DIM:],
    }


def _rope_tabs(pos0, t):
    inv = 1.0 / (ROPE_THETA ** (jnp.arange(0, MLA_ROPE, 2, dtype=F32) / MLA_ROPE))
    ang = (pos0 + jnp.arange(t, dtype=jnp.int32)).astype(F32)[:, None] * inv[None, :]
    cos, sin = jnp.cos(ang), jnp.sin(ang)
    z16 = jnp.zeros_like(cos)
    one = jnp.ones((t, MLA_NOPE), F32)
    z64 = jnp.zeros((t, MLA_NOPE), F32)
    z32 = jnp.zeros((t, LANES - MLA_NOPE - MLA_ROPE), F32)
    return (jnp.concatenate([one, cos, cos, z32], axis=1),
            jnp.concatenate([z64, -sin, z16, z32], axis=1),
            jnp.concatenate([z64, z16, sin, z32], axis=1))


def _to_blockdiag(s):
    b = s.shape[0]
    s = s.reshape(b, RWKV_HEADS // 2, 2, RWKV_HEAD, RWKV_HEAD)
    z = jnp.zeros_like(s[:, :, 0])
    top = jnp.concatenate([s[:, :, 0], z], axis=-1)
    bot = jnp.concatenate([z, s[:, :, 1]], axis=-1)
    return jnp.concatenate([top, bot], axis=-2)


def _from_blockdiag(sb):
    b = sb.shape[0]
    s0 = sb[:, :, :RWKV_HEAD, :RWKV_HEAD]
    s1 = sb[:, :, RWKV_HEAD:, RWKV_HEAD:]
    return jnp.stack([s0, s1], axis=2).reshape(b, RWKV_HEADS, RWKV_HEAD, RWKV_HEAD)


def _trunk(x, mod, pos0, caches, W, prep):
    b, t, _ = x.shape
    ckv_c, kpe_c, shift_c, wkv_c, pool_c, conv_c, gdn_c, ffn_c = caches
    tabs = _rope_tabs(pos0, t)
    o_ckv, o_kpe, o_shift, o_wkv, o_pool, o_conv, o_gdn, o_ffn = [], [], [], [], [], [], [], []
    for layer in range(DEPTH):
        i = layer // 2
        wl = prep[layer]
        mod_l = mod[layer]
        if layer % 2 == 0:
            p_mla, rw = _inproj(x, mod_l, W["norm_mix_g"][layer].reshape(1, -1), wl["w_in"])
            q, ckv_new, kpe_new, kpe_blk = _mla_prep(p_mla, tabs, wl)
            if ckv_c is None:
                ckv_all, kblk_all, past = ckv_new, kpe_blk, 0
            else:
                past = ckv_c.shape[2]
                ckv_all = jnp.concatenate([ckv_c[i], ckv_new], axis=1)
                past_blk = jnp.pad(kpe_c[i], ((0, 0), (0, 0), (MLA_NOPE, LANES - MLA_NOPE - MLA_ROPE)))
                kblk_all = jnp.concatenate([past_blk, kpe_blk], axis=1)
            kk, vv = _mla_kv(ckv_all, kblk_all, wl)
            m1 = _attention(q, kk, vv, past)
            m2, sblk = _rwkv(rw, shift_c[i][:, None, :], _to_blockdiag(wkv_c[i]), wl)
            o_ckv.append(ckv_new)
            o_kpe.append(kpe_new)
            o_shift.append(rw[:, -1])
            o_wkv.append(_from_blockdiag(sblk))
        else:
            u, qkv, z, ba = _inproj(x, mod_l, W["norm_mix_g"][layer].reshape(1, -1), wl["w_in"])
            hist16 = jnp.pad(pool_c[i], ((0, 0), (1, 0), (0, 0)))
            m1 = _pool(u, hist16, pos0, wl)
            m2, gs = _gdn(qkv, z, ba, conv_c[i], gdn_c[i], wl)
            o_pool.append(u[:, t - POOL_HIST:])
            o_conv.append(qkv[:, t - (GDN_CONV - 1):])
            o_gdn.append(gs)
        x, fc = _block(x, m1, m2, mod_l, W["norm_ffn_g"][layer].reshape(1, -1), ffn_c[layer],
                       wl["w_out1"], wl["w_out2"], wl["ffn_wg"], wl["ffn_wu"], W["ffn_conv_w"][layer], wl["ffn_wd"])
        o_ffn.append(fc)
    st = jnp.stack
    return x, (st(o_ckv), st(o_kpe), st(o_shift), st(o_wkv), st(o_pool), st(o_conv), st(o_gdn), st(o_ffn))


def _zero_caches(b, dt):
    n_even, n_odd = (DEPTH + 1) // 2, DEPTH // 2
    return (None, None,
            jnp.zeros((n_even, b, RWKV_COLS), dt),
            jnp.zeros((n_even, b, RWKV_HEADS, RWKV_HEAD, RWKV_HEAD), dt),
            jnp.zeros((n_odd, b, POOL_HIST, POOL_DIM), dt),
            jnp.zeros((n_odd, b, GDN_CONV - 1, GDN_QKV), dt),
            jnp.zeros((n_odd, b, GDN_HEADS, GDN_DK, GDN_DV), dt),
            jnp.zeros((DEPTH, b, FFN_CONV - 1, D_FF), dt))


def _run(x_prompt, x_sample, caches_s, c_prompt, c_sample, W):
    bp = x_prompt.shape[0]
    mod = _ada(jnp.concatenate([c_prompt, c_sample], axis=0), W["ada_w"], W["ada_b"])
    mod = mod.reshape(DEPTH, -1, 6, D_MODEL)
    prep = [(_prep_even if l % 2 == 0 else _prep_odd)(W, l // 2) for l in range(DEPTH)]
    for l in range(DEPTH):
        prep[l]["ffn_wg"] = W["ffn_w_gate"][l].astype(BF16)
        prep[l]["ffn_wu"] = W["ffn_w_up"][l].astype(BF16)
        prep[l]["ffn_wd"] = W["ffn_w_down"][l].astype(BF16)
    y_p, st_p = _trunk(x_prompt, mod[:, :bp], 0, _zero_caches(bp, x_prompt.dtype), W, prep)
    past_len = caches_s[0].shape[2]
    y_s, st_s = _trunk(x_sample, mod[:, bp:], past_len, caches_s, W, prep)
    return (y_p, y_s) + tuple(st_p) + tuple(st_s)


def kernel(x_prompt, x_sample, cache_mla_ckv, cache_mla_kpe, state_rwkv_shift, state_rwkv_wkv, state_pool,
           state_gdn_conv, state_gdn, state_ffn_conv, c_prompt, c_sample, ada_w, ada_b, norm_mix_g, norm_ffn_g,
           even_w_in, mla_g_qlat, mla_g_kvlat, mla_w_uq, mla_w_ukv, mla_g_qn, mla_g_qr, mla_g_kn, mla_g_kr,
           rwkv_mu, rwkv_w0, rwkv_w2, rwkv_a0, rwkv_a2, rwkv_g2, rwkv_k_k, rwkv_k_a, rwkv_r_k, rwkv_lnx_g,
           rwkv_lnx_b, even_w_out, odd_w_in, pool_w, pool_scale, gdn_conv_w, gdn_a_log, gdn_dt_bias, gdn_o_g,
           odd_w_out, ffn_w_gate, ffn_w_up, ffn_conv_w, ffn_w_down):
    W = {
        "ada_w": ada_w, "ada_b": ada_b, "norm_mix_g": norm_mix_g, "norm_ffn_g": norm_ffn_g,
        "even_w_in": even_w_in, "mla_g_qlat": mla_g_qlat, "mla_g_kvlat": mla_g_kvlat,
        "mla_w_uq": mla_w_uq, "mla_w_ukv": mla_w_ukv, "mla_g_qn": mla_g_qn, "mla_g_qr": mla_g_qr,
        "mla_g_kn": mla_g_kn, "mla_g_kr": mla_g_kr,
        "rwkv_mu": rwkv_mu, "rwkv_w0": rwkv_w0, "rwkv_w2": rwkv_w2, "rwkv_a0": rwkv_a0, "rwkv_a2": rwkv_a2,
        "rwkv_g2": rwkv_g2, "rwkv_k_k": rwkv_k_k, "rwkv_k_a": rwkv_k_a, "rwkv_r_k": rwkv_r_k,
        "rwkv_lnx_g": rwkv_lnx_g, "rwkv_lnx_b": rwkv_lnx_b, "even_w_out": even_w_out,
        "odd_w_in": odd_w_in, "pool_w": pool_w, "pool_scale": pool_scale, "gdn_conv_w": gdn_conv_w,
        "gdn_a_log": gdn_a_log, "gdn_dt_bias": gdn_dt_bias, "gdn_o_g": gdn_o_g, "odd_w_out": odd_w_out,
        "ffn_w_gate": ffn_w_gate, "ffn_w_up": ffn_w_up, "ffn_conv_w": ffn_conv_w, "ffn_w_down": ffn_w_down,
    }
    caches_s = (cache_mla_ckv, cache_mla_kpe, state_rwkv_shift, state_rwkv_wkv, state_pool, state_gdn_conv,
                state_gdn, state_ffn_conv)
    return _run(x_prompt, x_sample, caches_s, c_prompt, c_sample, W)
```
